```python
import math
import jax
import jax.numpy as jnp
from jax import lax
import numpy as np


D_MODEL = 2048
BATCH = 4
SEQ = 4096
DEPTH = 2

CTX_LEN = 256
GRID_W = 64
N_EVEN = (DEPTH + 1) // 2
N_ODD = DEPTH // 2
DEEPNORM_ALPHA = (2 * DEPTH) ** 0.25
DEEPNORM_BETA = (8 * DEPTH) ** -0.25
NORM_EPS = 1e-6
N_MOD = 6

S5_WIDTH = D_MODEL // 2
S5_GROUP = 16
S5_GROUPS = S5_WIDTH // S5_GROUP
S5_STATE = 64
S5_DT_MIN = 1e-3
S5_DT_MAX = 1e-1

DN_WIDTH = D_MODEL // 2
DN_HEAD_DIM = 128
DN_HEADS = DN_WIDTH // DN_HEAD_DIM
DN_CHUNK = 64
DN_CONV = 3
DN_DT_MIN = 1e-3
DN_DT_MAX = 1e-1

EVEN_COLS = (3 * DN_WIDTH, DN_WIDTH, S5_WIDTH, S5_WIDTH, 2 * DN_HEADS, 2 * DN_HEADS)
EVEN_IN = sum(EVEN_COLS)
EVEN_MIX = S5_WIDTH + DN_WIDTH

GLA_HEADS = 4
GLA_KEY_WIDTH = D_MODEL // 2
GLA_VAL_WIDTH = D_MODEL
GLA_DK = GLA_KEY_WIDTH // GLA_HEADS
GLA_DV = GLA_VAL_WIDTH // GLA_HEADS
GLA_RANK = 16
GLA_TAU = 16.0
GLA_CHUNK = 16
ODD_COLS = (GLA_KEY_WIDTH, GLA_KEY_WIDTH, GLA_VAL_WIDTH, GLA_VAL_WIDTH, 2 * GLA_RANK)
ODD_IN = sum(ODD_COLS)

FFN_HIDDEN = 5632
N_EXPERTS = 8
TOP_K = 2
EXPERT_HIDDEN = 7168
MOE_BLOCK = 512

F32 = jnp.float32

kernel_name = 'hybrid_s5_deltanet_gla_moe_dit'


def _cuts(cols):
    return np.cumsum(cols)[:-1].tolist()


def layer_norm(x, g, b):
    xf = x.astype(F32)
    mu = jnp.mean(xf, axis=-1, keepdims=True)
    var = jnp.mean(jnp.square(xf - mu), axis=-1, keepdims=True)
    return ((xf - mu) * lax.rsqrt(var + NORM_EPS) * g.astype(F32) + b.astype(F32)).astype(x.dtype)


def rms_norm(x, w):
    xf = x.astype(F32)
    return xf * lax.rsqrt(jnp.mean(jnp.square(xf), axis=-1, keepdims=True) + NORM_EPS) * w.astype(F32)


def l2_normalize(x):
    return x * lax.rsqrt(jnp.sum(jnp.square(x), axis=-1, keepdims=True) + NORM_EPS)


def modulate(h, shift, scale):
    return h * (1.0 + scale) + shift


def adaln(cond, w, b):
    return jnp.split(jax.nn.silu(cond) @ w + b, N_MOD, axis=-1)


def swiglu(h, w1, w3, w2):
    return (jax.nn.silu(h @ w1) * (h @ w3)) @ w2


def to_chunks(t, size):
    b, n = t.shape[0], t.shape[1]
    t = t.reshape((b, n // size, size) + t.shape[2:])
    return jnp.moveaxis(t, 3, 1)


def from_chunks(o):
    n, b, h, c, d = o.shape
    return jnp.transpose(o, (1, 0, 3, 2, 4)).reshape(b, n * c, h, d)


def run_bidirectional(rule, shared_x, shared_c, dirs_x, dirs_c, s0, need_ctx):
    outs_x, outs_c = [], []
    for d in range(2):
        flip = (lambda t: jnp.flip(t, axis=1)) if d == 1 else (lambda t: t)
        o_c, s_c = rule(*[flip(t) for t in shared_c + dirs_c[d]], s0)
        o_x, _ = rule(*[flip(t) for t in shared_x + dirs_x[d]], s_c)
        outs_x.append(flip(o_x))
        outs_c.append(flip(o_c))
    y_c = outs_c[0] + outs_c[1] if need_ctx else None
    return outs_x[0] + outs_x[1], y_c


def s5_discretize(lam_re, lam_im, log_step, b_re, b_im):
    delta = jnp.exp(log_step)[:, None]
    mag = jnp.exp(lam_re * delta)
    a_re = mag * jnp.cos(lam_im * delta)
    a_im = mag * jnp.sin(lam_im * delta)
    den = jnp.square(lam_re) + jnp.square(lam_im)
    f_re = ((a_re - 1.0) * lam_re + a_im * lam_im) / den
    f_im = (a_im * lam_re - (a_re - 1.0) * lam_im) / den
    bb_re = f_re[..., None] * b_re - f_im[..., None] * b_im
    bb_im = f_re[..., None] * b_im + f_im[..., None] * b_re
    return a_re, a_im, bb_re, bb_im


def linear_recurrence_combine(e1, e2):
    a1r, a1i, h1r, h1i = e1
    a2r, a2i, h2r, h2i = e2
    return (a2r * a1r - a2i * a1i, a2r * a1i + a2i * a1r,
            a2r * h1r - a2i * h1i + h2r, a2r * h1i + a2i * h1r + h2i)


def s5_states(u, a_re, a_im, bb_re, bb_im, h0, reverse):
    bu_re = jnp.einsum('btgh,gph->tbgp', u, bb_re)
    bu_im = jnp.einsum('btgh,gph->tbgp', u, bb_im)
    shape = (u.shape[1], 1) + a_re.shape
    ac_re, ac_im, h_re, h_im = lax.associative_scan(
        linear_recurrence_combine,
        (jnp.broadcast_to(a_re, shape), jnp.broadcast_to(a_im, shape), bu_re, bu_im),
        reverse=reverse, axis=0)
    if h0 is not None:
        h0_re, h0_im = h0
        h_re = h_re + ac_re * h0_re - ac_im * h0_im
        h_im = h_im + ac_re * h0_im + ac_im * h0_re
    return h_re, h_im


def s5_readout(h_re, h_im, c_re, c_im):
    return jnp.einsum('tbgp,ghp->btgh', h_re, c_re) - jnp.einsum('tbgp,ghp->btgh', h_im, c_im)


def s5_mixer(u_x, u_c, gate_x, gate_c, lam_re, lam_im, log_step, b_re, b_im, c_re, c_im, d_skip, need_ctx):
    lam_re, lam_im, log_step, b_re, b_im, c_re, c_im = [
        t.astype(F32) for t in (lam_re, lam_im, log_step, b_re, b_im, c_re, c_im)]

    def groups(u):
        return u.astype(F32).reshape(u.shape[0], u.shape[1], S5_GROUPS, S5_GROUP)

    ux, uc = groups(u_x), groups(u_c)
    ys_x, ys_c = [], []
    for d in range(2):
        rev = d == 1
        a_re, a_im, bb_re, bb_im = s5_discretize(lam_re[d], lam_im[d], log_step[d], b_re[d], b_im[d])
        hc_re, hc_im = s5_states(uc, a_re, a_im, bb_re, bb_im, None, rev)
        last = 0 if rev else -1
        hx_re, hx_im = s5_states(ux, a_re, a_im, bb_re, bb_im, (hc_re[last], hc_im[last]), rev)
        ys_x.append(s5_readout(hx_re, hx_im, c_re[d], c_im[d]))
        if need_ctx:
            ys_c.append(s5_readout(hc_re, hc_im, c_re[d], c_im[d]))

    def glu_out(y, u, gate):
        y = (y + d_skip.astype(F32).reshape(S5_GROUPS, S5_GROUP) * u).reshape(u.shape[0], u.shape[1], S5_WIDTH)
        return (jax.nn.gelu(y) * jax.nn.sigmoid(gate.astype(F32))).astype(gate.dtype)

    y_x = glu_out(ys_x[0] + ys_x[1], ux, gate_x)
    y_c = glu_out(ys_c[0] + ys_c[1], uc, gate_c) if need_ctx else None
    return y_x, y_c


def depthwise_conv2d(x, w, rows):
    b, n, ch = x.shape
    y = lax.conv_general_dilated(x.reshape(b, rows, GRID_W, ch), w[:, :, None, :], window_strides=(1, 1),
                                 padding='SAME', dimension_numbers=('NHWC', 'HWIO', 'NHWC'),
                                 feature_group_count=ch)
    return y.reshape(b, n, ch)


def depthwise_conv1d(x, w):
    ch = x.shape[-1]
    return lax.conv_general_dilated(x, w[:, None, :], window_strides=(1,), padding='SAME',
                                    dimension_numbers=('NWC', 'WIO', 'NWC'), feature_group_count=ch)


def gated_delta_rule(q, k, v, beta, g, s0):
    dv = v.shape[-1]
    q, k, v, beta, g = (to_chunks(t, DN_CHUNK) for t in (q, k, v, beta, g))
    gc = jnp.cumsum(g, axis=-1)
    idx = jnp.arange(DN_CHUNK)
    incl = idx[:, None] >= idx[None, :]
    strict = idx[:, None] > idx[None, :]
    decay = jnp.exp(jnp.where(incl, gc[..., :, None] - gc[..., None, :], -jnp.inf))
    kb = k * beta[..., None]
    m = jnp.eye(DN_CHUNK, dtype=F32) + jnp.where(strict, jnp.einsum('bhnid,bhnjd->bhnij', kb, k) * decay, 0.0)
    rhs = jnp.concatenate([v * beta[..., None], kb * jnp.exp(gc)[..., None]], axis=-1)
    sol = lax.linalg.triangular_solve(m, rhs, left_side=True, lower=True, unit_diagonal=True)
    u, w = sol[..., :dv], sol[..., dv:]
    attn = jnp.einsum('bhnid,bhnjd->bhnij', q, k) * decay
    qg = q * jnp.exp(gc)[..., None]
    kg = k * jnp.exp(gc[..., -1:] - gc)[..., None]
    g_last = jnp.exp(gc[..., -1])

    def step(s, xs):
        u_n, w_n, a_n, q_n, k_n, gl_n = xs
        v_new = u_n - jnp.einsum('bhck,bhkv->bhcv', w_n, s)
        o_n = jnp.einsum('bhck,bhkv->bhcv', q_n, s) + jnp.einsum('bhij,bhjv->bhiv', a_n, v_new)
        s = s * gl_n[..., None, None] + jnp.einsum('bhck,bhcv->bhkv', k_n, v_new)
        return s, o_n

    xs = tuple(jnp.moveaxis(t, 2, 0) for t in (u, w, attn, qg, kg, g_last))
    s_t, o = lax.scan(step, s0, xs)
    return from_chunks(o), s_t


def deltanet_mixer(qkv_x, qkv_c, z_x, z_c, beta_x, beta_c, a_x, a_c, conv_w, a_log, dt_bias, norm_w, need_ctx):
    rows = qkv_x.shape[1] // GRID_W
    qkv_x = jax.nn.silu(depthwise_conv2d(qkv_x, conv_w, rows))
    qkv_c = jax.nn.silu(depthwise_conv1d(qkv_c, conv_w[DN_CONV // 2]))
    a_log = a_log.astype(F32)
    dt_bias = dt_bias.astype(F32)

    def prep(qkv, beta, a):
        b, n = qkv.shape[0], qkv.shape[1]
        q, k, v = [t.reshape(b, n, DN_HEADS, DN_HEAD_DIM) for t in jnp.split(qkv.astype(F32), 3, axis=-1)]
        q = l2_normalize(q) * DN_HEAD_DIM ** -0.5
        k = l2_normalize(k)
        beta = jax.nn.sigmoid(beta.astype(F32))
        a = a.astype(F32)
        dirs = []
        for d in range(2):
            hs = slice(d * DN_HEADS, (d + 1) * DN_HEADS)
            g = -jnp.exp(a_log[d]) * jax.nn.softplus(a[..., hs] + dt_bias[d])
            dirs.append((beta[..., hs], g))
        return (q, k, v), dirs

    shared_x, dirs_x = prep(qkv_x, beta_x, a_x)
    shared_c, dirs_c = prep(qkv_c, beta_c, a_c)
    s0 = jnp.zeros((qkv_x.shape[0], DN_HEADS, DN_HEAD_DIM, DN_HEAD_DIM), F32)
    o_x, o_c = run_bidirectional(gated_delta_rule, shared_x, shared_c, dirs_x, dirs_c, s0, need_ctx)

    def gate_out(o, z):
        b, n = z.shape[0], z.shape[1]
        y = rms_norm(o, norm_w) * jax.nn.silu(z.astype(F32).reshape(b, n, DN_HEADS, DN_HEAD_DIM))
        return y.reshape(b, n, DN_WIDTH).astype(z.dtype)

    return gate_out(o_x, z_x), (gate_out(o_c, z_c) if need_ctx else None)


def even_mixer(hx, hc, w_in, lam_re, lam_im, log_step, b_re, b_im, c_re, c_im, d_skip,
               conv_w, a_log, dt_bias, norm_w, w_out, need_ctx):
    cuts = _cuts(EVEN_COLS)
    qkv_x, z_x, u_x, g_x, beta_x, a_x = jnp.split(hx @ w_in, cuts, axis=-1)
    qkv_c, z_c, u_c, g_c, beta_c, a_c = jnp.split(hc @ w_in, cuts, axis=-1)
    s5_x, s5_c = s5_mixer(u_x, u_c, g_x, g_c, lam_re, lam_im, log_step, b_re, b_im, c_re, c_im, d_skip, need_ctx)
    dn_x, dn_c = deltanet_mixer(qkv_x, qkv_c, z_x, z_c, beta_x, beta_c, a_x, a_c,
                                conv_w, a_log, dt_bias, norm_w, need_ctx)
    y_x = jnp.concatenate([s5_x, dn_x], axis=-1) @ w_out
    y_c = jnp.concatenate([s5_c, dn_c], axis=-1) @ w_out if need_ctx else None
    return y_x, y_c


def gla_rule(q, k, v, glog, s0):
    q, k, v, glog = (to_chunks(t, GLA_CHUNK) for t in (q, k, v, glog))
    cum = jnp.cumsum(glog, axis=3)
    q_in = q * jnp.exp(cum)
    k_in = k * jnp.exp(-cum)
    idx = jnp.arange(GLA_CHUNK)
    incl = idx[:, None] >= idx[None, :]
    attn = jnp.where(incl, jnp.einsum('bhnik,bhnjk->bhnij', q_in, k_in), 0.0)
    k_dec = k * jnp.exp(cum[..., -1:, :] - cum)
    g_last = jnp.exp(cum[..., -1, :])

    def step(s, xs):
        q_n, k_n, v_n, a_n, g_n = xs
        o_n = jnp.einsum('bhck,bhkv->bhcv', q_n, s) + jnp.einsum('bhij,bhjv->bhiv', a_n, v_n)
        s = s * g_n[..., None] + jnp.einsum('bhck,bhcv->bhkv', k_n, v_n)
        return s, o_n

    xs = tuple(jnp.moveaxis(t, 2, 0) for t in (q_in, k_dec, v, attn, g_last))
    s_t, o = lax.scan(step, s0, xs)
    return from_chunks(o), s_t


def odd_mixer(hx, hc, w_in, gate_w2, gate_b, norm_w, w_out, need_ctx):
    cuts = _cuts(ODD_COLS)
    gate_w2 = gate_w2.astype(F32)
    gate_b = gate_b.astype(F32)

    def prep(p):
        b, n = p.shape[0], p.shape[1]
        q, k, v, r, lr = jnp.split(p.astype(F32), cuts, axis=-1)
        q = q.reshape(b, n, GLA_HEADS, GLA_DK) * GLA_DK ** -0.5
        k = k.reshape(b, n, GLA_HEADS, GLA_DK)
        v = v.reshape(b, n, GLA_HEADS, GLA_DV)
        dirs = []
        for d in range(2):
            z = lr[..., d * GLA_RANK:(d + 1) * GLA_RANK] @ gate_w2[d] + gate_b[d]
            dirs.append((jax.nn.log_sigmoid(z).reshape(b, n, GLA_HEADS, GLA_DK) / GLA_TAU,))
        return (q, k, v), dirs, r

    shared_x, dirs_x, r_x = prep(hx @ w_in)
    shared_c, dirs_c, r_c = prep(hc @ w_in)
    s0 = jnp.zeros((hx.shape[0], GLA_HEADS, GLA_DK, GLA_DV), F32)
    o_x, o_c = run_bidirectional(gla_rule, shared_x, shared_c, dirs_x, dirs_c, s0, need_ctx)

    def gate_out(o, r):
        b, n = r.shape[0], r.shape[1]
        y = rms_norm(o, norm_w) * jax.nn.silu(r.reshape(b, n, GLA_HEADS, GLA_DV))
        return y.reshape(b, n, GLA_VAL_WIDTH).astype(hx.dtype) @ w_out

    return gate_out(o_x, r_x), (gate_out(o_c, r_c) if need_ctx else None)


def moe_swiglu(h, w_router, w1, w3, w2):
    n_tok, d_model = h.shape
    n_asg = n_tok * TOP_K
    logits = (h @ w_router).astype(F32)
    top_logit, top_e = lax.top_k(logits, TOP_K)
    gate = jax.nn.softmax(top_logit, axis=-1).astype(h.dtype).reshape(n_asg)
    flat_e = top_e.reshape(n_asg).astype(jnp.int32)
    flat_tok = jnp.repeat(jnp.arange(n_tok, dtype=jnp.int32), TOP_K)
    order = jnp.argsort(flat_e)
    sorted_e, sorted_tok, sorted_gate = flat_e[order], flat_tok[order], gate[order]
    counts = jnp.zeros((N_EXPERTS,), jnp.int32).at[flat_e].add(1)
    padded = (counts + MOE_BLOCK - 1) // MOE_BLOCK * MOE_BLOCK
    start = jnp.cumsum(counts) - counts
    padded_end = jnp.cumsum(padded)
    padded_start = padded_end - padded
    dest = padded_start[sorted_e] + jnp.arange(n_asg, dtype=jnp.int32) - start[sorted_e]
    n_blocks = -(-n_asg // MOE_BLOCK) + N_EXPERTS
    n_rows = n_blocks * MOE_BLOCK
    row_tok = jnp.zeros((n_rows,), jnp.int32).at[dest].set(sorted_tok)
    xb = h[row_tok].reshape(n_blocks, MOE_BLOCK, d_model)
    block_e = jnp.minimum(jnp.searchsorted(padded_end, jnp.arange(n_blocks, dtype=jnp.int32) * MOE_BLOCK,
                                           side='right'), N_EXPERTS - 1)

    def expert_block(args):
        xblk, e = args
        return swiglu(xblk, w1[e], w3[e], w2[e])

    yb = lax.map(expert_block, (xb, block_e)).reshape(n_rows, d_model)
    return jnp.zeros_like(h).at[sorted_tok].add(yb[dest] * sorted_gate[:, None])


def setup_inputs(seed: int = 0):
    key = jax.random.key(seed)
    keys = iter(jax.random.split(key, 40))

    def normal(shape, scale):
        return jax.random.normal(next(keys), shape, F32) * scale

    def uniform(shape, lo, hi):
        return jax.random.uniform(next(keys), shape, F32, lo, hi)

    D = D_MODEL
    s5_shape = (N_EVEN, 2, S5_GROUPS, S5_STATE)
    dn_dt = jnp.exp(uniform((N_EVEN, 2, DN_HEADS), math.log(DN_DT_MIN), math.log(DN_DT_MAX)))
    return {
        'x': normal((BATCH, SEQ, D), 1.0),
        'c': normal((BATCH, D), 1.0),
        'ctx': normal((BATCH, CTX_LEN, D), 1.0),
        'c_ctx': normal((D,), 1.0),
        'ada_w': normal((DEPTH, D, N_MOD * D), 0.5 * D ** -0.5),
        'ada_b': normal((DEPTH, N_MOD * D), 0.02),
        'ln1_g': 1.0 + normal((DEPTH, D), 0.02),
        'ln1_b': normal((DEPTH, D), 0.02),
        'ln2_g': 1.0 + normal((DEPTH, D), 0.02),
        'ln2_b': normal((DEPTH, D), 0.02),
        'e_w_in': normal((N_EVEN, D, EVEN_IN), D ** -0.5),
        's5_lam_re': -0.5 + normal(s5_shape, 0.01),
        's5_lam_im': jnp.pi * jnp.arange(S5_STATE, dtype=F32) + normal(s5_shape, 0.01),
        's5_log_step': uniform((N_EVEN, 2, S5_GROUPS), math.log(S5_DT_MIN), math.log(S5_DT_MAX)),
        's5_b_re': normal((N_EVEN, 2, S5_GROUPS, S5_STATE, S5_GROUP), (2 * S5_GROUP) ** -0.5),
        's5_b_im': normal((N_EVEN, 2, S5_GROUPS, S5_STATE, S5_GROUP), (2 * S5_GROUP) ** -0.5),
        's5_c_re': normal((N_EVEN, 2, S5_GROUPS, S5_GROUP, S5_STATE), 0.5),
        's5_c_im': normal((N_EVEN, 2, S5_GROUPS, S5_GROUP, S5_STATE), 0.5),
        's5_d': normal((N_EVEN, S5_WIDTH), 0.5),
        'dn_conv': normal((N_EVEN, DN_CONV, DN_CONV, 3 * DN_WIDTH), 1.0 / DN_CONV),
        'dn_a_log': jnp.log(uniform((N_EVEN, 2, DN_HEADS), 1.0, 16.0)),
        'dn_dt_bias': dn_dt + jnp.log(-jnp.expm1(-dn_dt)),
        'dn_norm_w': 1.0 + normal((N_EVEN, DN_HEAD_DIM), 0.02),
        'e_w_out': normal((N_EVEN, EVEN_MIX, D), DEEPNORM_BETA * EVEN_MIX ** -0.5),
        'ffn_w1': normal((N_EVEN, D, FFN_HIDDEN), D ** -0.5),
        'ffn_w3': normal((N_EVEN, D, FFN_HIDDEN), D ** -0.5),
        'ffn_w2': normal((N_EVEN, FFN_HIDDEN, D), DEEPNORM_BETA * FFN_HIDDEN ** -0.5),
        'o_w_in': normal((N_ODD, D, ODD_IN), D ** -0.5),
        'gla_w2': normal((N_ODD, 2, GLA_RANK, GLA_KEY_WIDTH), GLA_RANK ** -0.5),
        'gla_b': normal((N_ODD, 2, GLA_KEY_WIDTH), 0.1),
        'gla_norm_w': 1.0 + normal((N_ODD, GLA_DV), 0.02),
        'o_w_out': normal((N_ODD, GLA_VAL_WIDTH, D), DEEPNORM_BETA * GLA_VAL_WIDTH ** -0.5),
        'moe_router': normal((N_ODD, D, N_EXPERTS), D ** -0.5),
        'moe_w1': normal((N_ODD, N_EXPERTS, D, EXPERT_HIDDEN), D ** -0.5),
        'moe_w3': normal((N_ODD, N_EXPERTS, D, EXPERT_HIDDEN), D ** -0.5),
        'moe_w2': normal((N_ODD, N_EXPERTS, EXPERT_HIDDEN, D), DEEPNORM_BETA * EXPERT_HIDDEN ** -0.5),
    }


def reference(x, c, ctx, c_ctx, ada_w, ada_b, ln1_g, ln1_b, ln2_g, ln2_b,
              e_w_in, s5_lam_re, s5_lam_im, s5_log_step, s5_b_re, s5_b_im,
              s5_c_re, s5_c_im, s5_d, dn_conv, dn_a_log, dn_dt_bias, dn_norm_w,
              e_w_out, ffn_w1, ffn_w3, ffn_w2, o_w_in, gla_w2, gla_b, gla_norm_w,
              o_w_out, moe_router, moe_w1, moe_w3, moe_w2):
    n_lat = x.shape[0] * x.shape[1]
    for i in range(DEPTH):
        j = i // 2
        need_ctx = i < DEPTH - 1
        sh1, sc1, gt1, sh2, sc2, gt2 = [m[:, None, :] for m in adaln(c, ada_w[i], ada_b[i])]
        csh1, csc1, cgt1, csh2, csc2, cgt2 = adaln(c_ctx, ada_w[i], ada_b[i])
        hx = modulate(x, sh1, sc1)
        hc = modulate(ctx, csh1, csc1)
        if i % 2 == 0:
            mix_x, mix_c = even_mixer(hx, hc, e_w_in[j], s5_lam_re[j], s5_lam_im[j], s5_log_step[j],
                                      s5_b_re[j], s5_b_im[j], s5_c_re[j], s5_c_im[j], s5_d[j],
                                      dn_conv[j], dn_a_log[j], dn_dt_bias[j], dn_norm_w[j], e_w_out[j], need_ctx)
        else:
            mix_x, mix_c = odd_mixer(hx, hc, o_w_in[j], gla_w2[j], gla_b[j], gla_norm_w[j], o_w_out[j], need_ctx)
        x = layer_norm(DEEPNORM_ALPHA * x + gt1 * mix_x, ln1_g[i], ln1_b[i])
        tokens = [modulate(x, sh2, sc2).reshape(n_lat, D_MODEL)]
        if need_ctx:
            ctx = layer_norm(DEEPNORM_ALPHA * ctx + cgt1 * mix_c, ln1_g[i], ln1_b[i])
            tokens.append(modulate(ctx, csh2, csc2).reshape(-1, D_MODEL))
        tokens = jnp.concatenate(tokens, axis=0)
        if i % 2 == 0:
            f = swiglu(tokens, ffn_w1[j], ffn_w3[j], ffn_w2[j])
        else:
            f = moe_swiglu(tokens, moe_router[j], moe_w1[j], moe_w3[j], moe_w2[j])
        x = layer_norm(DEEPNORM_ALPHA * x + gt2 * f[:n_lat].reshape(x.shape), ln2_g[i], ln2_b[i])
        if need_ctx:
            ctx = layer_norm(DEEPNORM_ALPHA * ctx + cgt2 * f[n_lat:].reshape(ctx.shape), ln2_g[i], ln2_b[i])
    return x
```

```python
import functools
import math

import jax
import jax.numpy as jnp
from jax import lax
from jax.experimental import pallas as pl
from jax.experimental.pallas import tpu as pltpu

F32 = jnp.float32
BF16 = jnp.bfloat16

LANES = 128
VMEM_LIMIT = 60 * 1024 * 1024

D_MODEL = 2048
DEPTH = 2
ALPHA = (2 * DEPTH) ** 0.25
NORM_EPS = 1e-6
N_MOD = 6
GRID_W = 64

S5_WIDTH = D_MODEL // 2
S5_GROUP = 16
S5_GROUPS = S5_WIDTH // S5_GROUP
S5_STATE = 64
S5_L = 16
S5_TILES = S5_WIDTH // LANES
S5_GPT = LANES // S5_GROUP
S5_SW = S5_GPT * S5_STATE

DN_WIDTH = D_MODEL // 2
DN_HEAD_DIM = 128
DN_HEADS = DN_WIDTH // DN_HEAD_DIM
DN_CHUNK = 64
QKV_TILES = 3 * DN_WIDTH // LANES

GLA_HEADS = 4
GLA_DK = 256
GLA_DV = 512
GLA_RANK = 16
GLA_TAU = 16.0
GLA_CHUNK = 16

MAIN_COLS = 6144
FFN_HIDDEN = 5632
N_EXPERTS = 8
TOP_K = 2
EXPERT_HIDDEN = 7168
MOE_TM = 1024
ROW_BLK = 256
NEG_BIG = -1e30

NT_DIMS = (((1,), (1,)), ((), ()))
TN_DIMS = (((0,), (0,)), ((), ()))


def _cparams(*sem):
    return pltpu.CompilerParams(dimension_semantics=sem, vmem_limit_bytes=VMEM_LIMIT)


def _dot(a, b, dims=None):
    a = a.astype(BF16)
    b = b.astype(BF16)
    if dims is None:
        return jnp.dot(a, b, preferred_element_type=F32)
    return lax.dot_general(a, b, dims, preferred_element_type=F32)


def _split(a):
    hi = a.astype(BF16)
    lo = (a - hi.astype(F32)).astype(BF16)
    return hi, lo


def _dot3(a, b, dims=None):
    ah, al = _split(a)
    bh, bl = _split(b)
    return _dot(ah, bh, dims) + _dot(al, bh, dims) + _dot(ah, bl, dims)


def _tri_dot(tri, g):
    g1 = g.astype(BF16)
    r1 = g - g1.astype(F32)
    g2 = r1.astype(BF16)
    g3 = (r1 - g2.astype(F32)).astype(BF16)
    t = tri.astype(BF16)
    return (jnp.dot(t, g1, preferred_element_type=F32) + jnp.dot(t, g2, preferred_element_type=F32)
            + jnp.dot(t, g3, preferred_element_type=F32))


def _silu(x):
    return x * jax.nn.sigmoid(x)


def _softplus(x):
    return jnp.maximum(x, 0.0) + jnp.log1p(jnp.exp(-jnp.abs(x)))


def _adaln_kernel(c_ref, w_ref, b_ref, o_ref):
    o_ref[0] = _dot3(_silu(c_ref[...]), w_ref[0]) + b_ref[0]


def _adaln(c8, ada_w, ada_b):
    tn = 1024
    n = N_MOD * D_MODEL
    return pl.pallas_call(
        _adaln_kernel,
        grid=(DEPTH, n // tn),
        in_specs=[pl.BlockSpec((8, D_MODEL), lambda l, j: (0, 0)),
                  pl.BlockSpec((1, D_MODEL, tn), lambda l, j: (l, 0, j)),
                  pl.BlockSpec((1, 1, tn), lambda l, j: (l, 0, j))],
        out_specs=pl.BlockSpec((1, 8, tn), lambda l, j: (l, 0, j)),
        out_shape=jax.ShapeDtypeStruct((DEPTH, 8, n), F32),
        compiler_params=_cparams("arbitrary", "arbitrary"),
        name="adaln",
    )(c8, ada_w, ada_b.reshape(DEPTH, 1, n))


def _mod_row(nb, nx_blocks):
    return lambda b, r: jnp.where(r < nx_blocks, b, nb)


def _mod_kernel(x_ref, m_ref, o_ref):
    sh = m_ref[0, 0, 0:1, :]
    sc = m_ref[0, 0, 1:2, :]
    o_ref[0] = (x_ref[0] * (1.0 + sc) + sh).astype(o_ref.dtype)


def _modulate(xc, mods, layer, nx):
    nb, rows, _ = xc.shape
    row = _mod_row(nb, nx // ROW_BLK)
    return pl.pallas_call(
        _mod_kernel,
        grid=(nb, rows // ROW_BLK),
        in_specs=[pl.BlockSpec((1, ROW_BLK, D_MODEL), lambda b, r: (b, r, 0)),
                  pl.BlockSpec((1, 1, N_MOD, D_MODEL), lambda b, r: (layer, row(b, r), 0, 0))],
        out_specs=pl.BlockSpec((1, ROW_BLK, D_MODEL), lambda b, r: (b, r, 0)),
        out_shape=jax.ShapeDtypeStruct(xc.shape, BF16),
        compiler_params=_cparams("arbitrary", "arbitrary"),
        name="modulate",
    )(xc, mods)


def _ln_kernel(*refs, gate_row, moe, emit_h):
    it = iter(refs)
    x_ref, y_ref = next(it), next(it)
    gw_ref = next(it) if moe else None
    mg_ref, mn_ref, g_ref, b_ref, xo_ref = next(it), next(it), next(it), next(it), next(it)
    x = x_ref[0]
    if moe:
        gw = gw_ref[0]
        y = gw[:, 0:1] * y_ref[0, :, 0:D_MODEL] + gw[:, 1:2] * y_ref[0, :, D_MODEL:2 * D_MODEL]
    else:
        y = y_ref[0].astype(F32)
    v = ALPHA * x + mg_ref[0, 0, gate_row:gate_row + 1, :] * y
    mu = jnp.mean(v, axis=-1, keepdims=True)
    vc = v - mu
    var = jnp.mean(vc * vc, axis=-1, keepdims=True)
    xn = vc * lax.rsqrt(var + NORM_EPS) * g_ref[...] + b_ref[...]
    xo_ref[0] = xn
    if emit_h:
        ho_ref = next(it)
        sh = mn_ref[0, 0, 0:1, :]
        sc = mn_ref[0, 0, 1:2, :]
        ho_ref[0] = (xn * (1.0 + sc) + sh).astype(ho_ref.dtype)


def _resid_ln(xc, y, mods, gate_layer, gate_row, next_layer, next_row0, ln_g, ln_b, nx, rows_out,
              h_dtype=None, moe_gates=None):
    nb = xc.shape[0]
    row = _mod_row(nb, nx // ROW_BLK)
    moe = moe_gates is not None
    emit_h = h_dtype is not None
    mods_next = mods[:, :, next_row0:next_row0 + 2, :] if emit_h else mods[:, :, 0:2, :]
    in_specs = [pl.BlockSpec((1, ROW_BLK, D_MODEL), lambda b, r: (b, r, 0)),
                pl.BlockSpec((1, ROW_BLK, y.shape[-1]), lambda b, r: (b, r, 0))]
    args = [xc, y]
    if moe:
        in_specs.append(pl.BlockSpec((1, ROW_BLK, TOP_K), lambda b, r: (b, r, 0)))
        args.append(moe_gates)
    in_specs += [pl.BlockSpec((1, 1, N_MOD, D_MODEL), lambda b, r: (gate_layer, row(b, r), 0, 0)),
                 pl.BlockSpec((1, 1, 2, D_MODEL), lambda b, r: (next_layer, row(b, r), 0, 0)),
                 pl.BlockSpec((1, D_MODEL), lambda b, r: (0, 0)),
                 pl.BlockSpec((1, D_MODEL), lambda b, r: (0, 0))]
    args += [mods, mods_next, ln_g.reshape(1, D_MODEL), ln_b.reshape(1, D_MODEL)]
    out_shape = [jax.ShapeDtypeStruct((nb, rows_out, D_MODEL), F32)]
    out_specs = [pl.BlockSpec((1, ROW_BLK, D_MODEL), lambda b, r: (b, r, 0))]
    if emit_h:
        out_shape.append(jax.ShapeDtypeStruct((nb, rows_out, D_MODEL), h_dtype))
        out_specs.append(pl.BlockSpec((1, ROW_BLK, D_MODEL), lambda b, r: (b, r, 0)))
    res = pl.pallas_call(
        functools.partial(_ln_kernel, gate_row=gate_row, moe=moe, emit_h=emit_h),
        grid=(nb, rows_out // ROW_BLK),
        in_specs=in_specs, out_specs=out_specs, out_shape=out_shape,
        compiler_params=_cparams("arbitrary", "arbitrary"),
        name="resid_ln",
    )(*args)
    return res if emit_h else (res[0], None)


def _mm_in_kernel(a_ref, w_ref, ws_ref, o_ref, os_ref, *, tn):
    a = a_ref[...]
    res = jnp.dot(a, w_ref[...].astype(BF16), preferred_element_type=F32)
    for t in range(tn // LANES):
        o_ref[t] = res[:, t * LANES:(t + 1) * LANES].astype(o_ref.dtype)

    @pl.when(pl.program_id(1) == 0)
    def _():
        os_ref[...] = jnp.dot(a, ws_ref[...].astype(BF16), preferred_element_type=F32)


def _row_tile(m, cap):
    best = 16
    for t in range(16, cap + 1, 16):
        if m % t == 0:
            best = t
    return best


def _mm_in(a, w):
    m = a.shape[0]
    tm = _row_tile(m, 1088)
    tn = 512
    w_small = jnp.pad(w[:, MAIN_COLS:], ((0, 0), (0, LANES - (w.shape[1] - MAIN_COLS))))
    return pl.pallas_call(
        functools.partial(_mm_in_kernel, tn=tn),
        grid=(m // tm, MAIN_COLS // tn),
        in_specs=[pl.BlockSpec((tm, D_MODEL), lambda i, j: (i, 0)),
                  pl.BlockSpec((D_MODEL, tn), lambda i, j: (0, j)),
                  pl.BlockSpec((D_MODEL, LANES), lambda i, j: (0, 0))],
        out_specs=[pl.BlockSpec((tn // LANES, tm, LANES), lambda i, j: (j, i, 0)),
                   pl.BlockSpec((tm, LANES), lambda i, j: (i, 0))],
        out_shape=[jax.ShapeDtypeStruct((MAIN_COLS // LANES, m, LANES), BF16),
                   jax.ShapeDtypeStruct((m, LANES), F32)],
        compiler_params=_cparams("arbitrary", "arbitrary"),
        name="mm_in",
    )(a, w, w_small)


def _mm_out_kernel(*refs, n_a):
    a_refs, w_ref, o_ref = refs[:n_a], refs[n_a], refs[n_a + 1]
    a = jnp.concatenate([r[t] for r in a_refs for t in range(r.shape[0])], axis=1)
    o_ref[...] = jnp.dot(a, w_ref[...].astype(BF16), preferred_element_type=F32)


def _mm_out(a_list, w):
    m = a_list[0].shape[1]
    tm = _row_tile(m, 1088)
    tn = 512
    in_specs = [pl.BlockSpec((a.shape[0], tm, LANES), lambda i, j: (0, i, 0)) for a in a_list]
    in_specs.append(pl.BlockSpec((D_MODEL, tn), lambda i, j: (0, j)))
    return pl.pallas_call(
        functools.partial(_mm_out_kernel, n_a=len(a_list)),
        grid=(m // tm, D_MODEL // tn),
        in_specs=in_specs,
        out_specs=pl.BlockSpec((tm, tn), lambda i, j: (i, j)),
        out_shape=jax.ShapeDtypeStruct((m, D_MODEL), F32),
        compiler_params=_cparams("arbitrary", "arbitrary"),
        name="mm_out",
    )(*a_list, w)


def _ffn_kernel(a_ref, w1_ref, w3_ref, w2_ref, o_ref):
    j = pl.program_id(1)
    a = a_ref[...]
    h1 = jnp.dot(a, w1_ref[...].astype(BF16), preferred_element_type=F32)
    h3 = jnp.dot(a, w3_ref[...].astype(BF16), preferred_element_type=F32)
    hh = (_silu(h1) * h3).astype(BF16)
    part = jnp.dot(hh, w2_ref[...].astype(BF16), preferred_element_type=F32)

    @pl.when(j == 0)
    def _():
        o_ref[...] = part

    @pl.when(j > 0)
    def _():
        o_ref[...] += part


def _ffn(a, w1, w3, w2):
    m = a.shape[0]
    hidden = w1.shape[1]
    tm = _row_tile(m, 1088)
    th = 256
    return pl.pallas_call(
        _ffn_kernel,
        grid=(m // tm, hidden // th),
        in_specs=[pl.BlockSpec((tm, D_MODEL), lambda i, j: (i, 0)),
                  pl.BlockSpec((D_MODEL, th), lambda i, j: (0, j)),
                  pl.BlockSpec((D_MODEL, th), lambda i, j: (0, j)),
                  pl.BlockSpec((th, D_MODEL), lambda i, j: (j, 0))],
        out_specs=pl.BlockSpec((tm, D_MODEL), lambda i, j: (i, 0)),
        out_shape=jax.ShapeDtypeStruct((m, D_MODEL), F32),
        compiler_params=_cparams("arbitrary", "arbitrary"),
        name="ffn",
    )(a, w1, w3, w2)


def _s5_tables(lam_re, lam_im, log_step, b_re, b_im, c_re, c_im, d_skip):
    hp = lax.Precision.HIGHEST
    tau = jnp.arange(S5_L + 1, dtype=F32)[:, None, None]
    lag_k, ab, ca, a_l = [], [], [], []
    for d in range(2):
        delta = jnp.exp(log_step[d])[:, None]
        lr, li = lam_re[d], lam_im[d]
        mag = jnp.exp(lr * delta * tau)
        ang = li * delta * tau
        are, aim = mag * jnp.cos(ang), mag * jnp.sin(ang)
        den = jnp.square(lr) + jnp.square(li)
        f_re = ((are[1] - 1.0) * lr + aim[1] * li) / den
        f_im = (aim[1] * lr - (are[1] - 1.0) * li) / den
        bbr = f_re[..., None] * b_re[d] - f_im[..., None] * b_im[d]
        bbi = f_re[..., None] * b_im[d] + f_im[..., None] * b_re[d]
        car = c_re[d][None] * are[:, :, None, :] - c_im[d][None] * aim[:, :, None, :]
        cai = c_re[d][None] * aim[:, :, None, :] + c_im[d][None] * are[:, :, None, :]
        lag_k.append(jnp.einsum('tghp,gpk->tghk', car, bbr, precision=hp)
                     - jnp.einsum('tghp,gpk->tghk', cai, bbi, precision=hp))
        ab.append((are[..., None] * bbr[None] - aim[..., None] * bbi[None],
                   are[..., None] * bbi[None] + aim[..., None] * bbr[None]))
        ca.append((car, cai))
        a_l.append((are[S5_L], aim[S5_L]))
    eye = jnp.eye(S5_GPT, dtype=F32)
    kf, kb = lag_k
    lag = jnp.concatenate([kb[S5_L - 1:0:-1], (kf[0] + kb[0])[None], kf[1:S5_L]], axis=0)
    lag = lag.reshape(2 * S5_L - 1, S5_TILES, S5_GPT, S5_GROUP, S5_GROUP)
    wlag = jnp.einsum('ljghk,gf->jgklfh', lag, eye).reshape(S5_TILES, LANES, (2 * S5_L - 1) * LANES)
    wb = jnp.stack([ab[0][0][:S5_L][::-1], ab[0][1][:S5_L][::-1], ab[1][0][:S5_L], ab[1][1][:S5_L]], axis=0)
    wb = wb.reshape(4, S5_L, S5_TILES, S5_GPT, S5_STATE, S5_GROUP)
    wb = jnp.einsum('csjgpk,gf->jsgkcfp', wb, eye).reshape(S5_TILES, S5_L * LANES, 4 * S5_SW)
    wc = jnp.stack([ca[0][0][1:], -ca[0][1][1:], ca[1][0][S5_L:0:-1], -ca[1][1][S5_L:0:-1]], axis=0)
    wc = wc.reshape(4, S5_L, S5_TILES, S5_GPT, S5_GROUP, S5_STATE)
    wc = jnp.einsum('ctjghp,gf->jcfptgh', wc, eye).reshape(S5_TILES, 4 * S5_SW, S5_L * LANES)
    a16 = jnp.stack([a_l[0][0], a_l[0][1], a_l[1][0], a_l[1][1]], axis=0)
    a16 = a16.reshape(4, S5_TILES, S5_SW).transpose(1, 0, 2)
    dsk = jnp.tile(d_skip.astype(F32).reshape(S5_TILES, 1, LANES), (1, 1, S5_L))
    return wlag.astype(BF16), wb.astype(BF16), wc.astype(BF16), a16, dsk


def _s5_kernel(u_ref, g_ref, wlag_ref, wb_ref, wc_ref, a16_ref, d_ref, o_ref,
               wt_ref, s_ref, hin_ref, *, n_x, n_c):
    rows = n_x + n_c

    @pl.when(pl.program_id(1) == 0)
    def _():
        for s in range(S5_L):
            wt_ref[s * LANES:(s + 1) * LANES, :] = wlag_ref[0, :, (S5_L - 1 - s) * LANES:(2 * S5_L - 1 - s) * LANES]

    ub = u_ref[0]
    u = ub.astype(F32)
    s_ref[...] = jnp.dot(ub, wb_ref[0], preferred_element_type=F32)
    a = a16_ref[0]
    afr, afi, abr, abi = a[0:1], a[1:2], a[2:3], a[3:4]
    w = S5_SW

    def body(i, carry):
        hfr, hfi, hbr, hbi = carry
        rf = jnp.where(i < n_c, n_x + i, i - n_c)
        rb = rows - 1 - i
        hin_ref[pl.ds(rf, 1), 0:w] = hfr
        hin_ref[pl.ds(rf, 1), w:2 * w] = hfi
        hin_ref[pl.ds(rb, 1), 2 * w:3 * w] = hbr
        hin_ref[pl.ds(rb, 1), 3 * w:4 * w] = hbi
        sfr = s_ref[pl.ds(rf, 1), 0:w]
        sfi = s_ref[pl.ds(rf, 1), w:2 * w]
        sbr = s_ref[pl.ds(rb, 1), 2 * w:3 * w]
        sbi = s_ref[pl.ds(rb, 1), 3 * w:4 * w]
        return (afr * hfr - afi * hfi + sfr, afr * hfi + afi * hfr + sfi,
                abr * hbr - abi * hbi + sbr, abr * hbi + abi * hbr + sbi)

    z = jnp.zeros((1, w), F32)
    lax.fori_loop(0, rows, body, (z, z, z, z))
    y = (jnp.dot(ub, wt_ref[...], preferred_element_type=F32)
         + jnp.dot(hin_ref[...].astype(BF16), wc_ref[0], preferred_element_type=F32))
    y = y + d_ref[0] * u
    gelu = 0.5 * y * (1.0 + jnp.tanh(math.sqrt(2.0 / math.pi) * (y + 0.044715 * (y * y * y))))
    o_ref[0] = (gelu * jax.nn.sigmoid(g_ref[0].astype(F32))).astype(o_ref.dtype)


def _s5_mixer(p, tables, nb, n_x, n_c, u_tile0, g_tile0):
    wlag, wb, wc, a16, dsk = tables
    rows = n_x + n_c
    width = S5_L * LANES
    pv = p.reshape(p.shape[0], nb * rows, width)
    return pl.pallas_call(
        functools.partial(_s5_kernel, n_x=n_x, n_c=n_c),
        grid=(S5_TILES, nb),
        in_specs=[pl.BlockSpec((1, rows, width), lambda j, b: (u_tile0 + j, b, 0)),
                  pl.BlockSpec((1, rows, width), lambda j, b: (g_tile0 + j, b, 0)),
                  pl.BlockSpec((1, LANES, wlag.shape[2]), lambda j, b: (j, 0, 0), pipeline_mode=pl.Buffered(1)),
                  pl.BlockSpec((1, width, 4 * S5_SW), lambda j, b: (j, 0, 0), pipeline_mode=pl.Buffered(1)),
                  pl.BlockSpec((1, 4 * S5_SW, width), lambda j, b: (j, 0, 0), pipeline_mode=pl.Buffered(1)),
                  pl.BlockSpec((1, 4, S5_SW), lambda j, b: (j, 0, 0)),
                  pl.BlockSpec((1, 1, width), lambda j, b: (j, 0, 0))],
        out_specs=pl.BlockSpec((1, rows, width), lambda j, b: (j, b, 0)),
        out_shape=jax.ShapeDtypeStruct((S5_TILES, nb * rows, width), BF16),
        scratch_shapes=[pltpu.VMEM((width, width), BF16),
                        pltpu.VMEM((rows, 4 * S5_SW), F32),
                        pltpu.VMEM((rows, 4 * S5_SW), F32)],
        compiler_params=_cparams("arbitrary", "arbitrary"),
        name="s5",
    )(pv, pv, wlag, wb, wc, a16, dsk).reshape(S5_TILES, nb * rows * S5_L, LANES)


def _dn_conv_kernel(x_ref, w_ref, o_ref, xp_ref, cp_ref, *, n_x, n_c):
    pad = 72
    blk = 512
    tile = pl.program_id(0)
    w = w_ref[...]
    xp_ref[0:pad, :] = jnp.zeros((pad, LANES), F32)
    xp_ref[pad + n_x:pad + n_x + pad, :] = jnp.zeros((pad, LANES), F32)
    xp_ref[pad:pad + n_x, :] = x_ref[0, 0:n_x, :].astype(F32)
    cp_ref[0:8, :] = jnp.zeros((8, LANES), F32)
    cp_ref[8 + n_c:16 + n_c, :] = jnp.zeros((8, LANES), F32)
    cp_ref[8:8 + n_c, :] = x_ref[0, n_x:n_x + n_c, :].astype(F32)
    is_qk = tile < 2 * DN_HEADS
    scale = jnp.where(tile < DN_HEADS, DN_HEAD_DIM ** -0.5, 1.0)

    def finish(y):
        y = _silu(y)
        nrm = y * lax.rsqrt(jnp.sum(y * y, axis=-1, keepdims=True) + NORM_EPS) * scale
        return jnp.where(is_qk, nrm, y)

    col = lax.broadcasted_iota(jnp.int32, (blk, LANES), 0) & (GRID_W - 1)
    not_first = col != 0
    not_last = col != GRID_W - 1
    for r0 in range(0, n_x, blk):
        left = mid = right = None
        for kh in range(3):
            base = pad + r0 + (kh - 1) * GRID_W
            tl = xp_ref[base - 1:base - 1 + blk, :] * w[3 * kh:3 * kh + 1, :]
            tm = xp_ref[base:base + blk, :] * w[3 * kh + 1:3 * kh + 2, :]
            tr = xp_ref[base + 1:base + 1 + blk, :] * w[3 * kh + 2:3 * kh + 3, :]
            left = tl if left is None else left + tl
            mid = tm if mid is None else mid + tm
            right = tr if right is None else right + tr
        y = jnp.where(not_first, left, 0.0) + mid + jnp.where(not_last, right, 0.0)
        o_ref[0, r0:r0 + blk, :] = finish(y)
    yc = (cp_ref[7:7 + n_c, :] * w[3:4, :] + cp_ref[8:8 + n_c, :] * w[4:5, :]
          + cp_ref[9:9 + n_c, :] * w[5:6, :])
    o_ref[0, n_x:n_x + n_c, :] = finish(yc)


def _dn_conv(p, conv_w, nb, n_x, n_c):
    rows = n_x + n_c
    w9 = conv_w.reshape(9, 3 * DN_WIDTH)
    return pl.pallas_call(
        functools.partial(_dn_conv_kernel, n_x=n_x, n_c=n_c),
        grid=(QKV_TILES, nb),
        in_specs=[pl.BlockSpec((1, rows, LANES), lambda t, b: (t, b, 0)),
                  pl.BlockSpec((9, LANES), lambda t, b: (0, t))],
        out_specs=pl.BlockSpec((1, rows, LANES), lambda t, b: (t, b, 0)),
        out_shape=jax.ShapeDtypeStruct((QKV_TILES, nb * rows, LANES), F32),
        scratch_shapes=[pltpu.VMEM((n_x + 144, LANES), F32), pltpu.VMEM((n_c + 16, LANES), F32)],
        compiler_params=_cparams("arbitrary", "arbitrary"),
        name="dn_conv",
    )(p, w9)


def _dn_gates_kernel(s_ref, coef_ref, dtb_ref, o_ref, *, tm):
    s = s_ref[...]
    lane = lax.broadcasted_iota(jnp.int32, (tm, LANES), 1)
    beta = jax.nn.sigmoid(s)
    g = coef_ref[...] * _softplus(s + dtb_ref[...])
    o_ref[...] = jnp.where(lane < 2 * DN_HEADS, beta, g)
    ri = lax.broadcasted_iota(jnp.int32, (DN_CHUNK, DN_CHUNK), 0)
    ci = lax.broadcasted_iota(jnp.int32, (DN_CHUNK, DN_CHUNK), 1)
    lower = (ri >= ci).astype(F32)
    upper = (ri <= ci).astype(F32)
    lane_c = lax.broadcasted_iota(jnp.int32, (DN_CHUNK, LANES), 1)
    for c0 in range(0, tm, DN_CHUNK):
        gc = g[c0:c0 + DN_CHUNK, :]
        gsh = pltpu.roll(gc, 2 * DN_HEADS, 1)
        cf = _tri_dot(lower, gsh)
        cb = _tri_dot(upper, gsh)
        cum = jnp.where(lane_c < 5 * DN_HEADS, cf, cb)
        cur = o_ref[c0:c0 + DN_CHUNK, :]
        o_ref[c0:c0 + DN_CHUNK, :] = jnp.where((lane_c >= 4 * DN_HEADS) & (lane_c < 6 * DN_HEADS), cum, cur)


def _dn_gates(small, a_log, dt_bias):
    m = small.shape[0]
    tm = _row_tile(m, 1088)
    coef = jnp.zeros((LANES,), F32).at[2 * DN_HEADS:4 * DN_HEADS].set(-jnp.exp(a_log.astype(F32)).reshape(-1))
    dtb = jnp.zeros((LANES,), F32).at[2 * DN_HEADS:4 * DN_HEADS].set(dt_bias.astype(F32).reshape(-1))
    return pl.pallas_call(
        functools.partial(_dn_gates_kernel, tm=tm),
        grid=(m // tm,),
        in_specs=[pl.BlockSpec((tm, LANES), lambda i: (i, 0)),
                  pl.BlockSpec((1, LANES), lambda i: (0, 0)),
                  pl.BlockSpec((1, LANES), lambda i: (0, 0))],
        out_specs=pl.BlockSpec((tm, LANES), lambda i: (i, 0)),
        out_shape=jax.ShapeDtypeStruct((m, LANES), F32),
        compiler_params=_cparams("arbitrary"),
        name="dn_gates",
    )(small, coef.reshape(1, LANES), dtb.reshape(1, LANES))


def _neumann_inverse(lm):
    n = lm.shape[0]
    eye = (lax.broadcasted_iota(jnp.int32, (n, n), 0) == lax.broadcasted_iota(jnp.int32, (n, n), 1)).astype(F32)
    inv = eye - lm
    pw = _dot3(lm, lm)
    for _ in range(4):
        inv = inv + _dot3(inv, pw)
        pw = _dot3(pw, pw)
    return inv + _dot3(inv, pw)


def _chunk_masks(n):
    ri = lax.broadcasted_iota(jnp.int32, (n, n), 0)
    ci = lax.broadcasted_iota(jnp.int32, (n, n), 1)
    return ((ri > ci, ri >= ci), (ri < ci, ri <= ci))


DN_CG = 4


def _dn_intra_kernel(k_ref, v_ref, col_ref, row_ref, u_ref, w_ref):
    masks = _chunk_masks(DN_CHUNK)
    for c in range(DN_CG):
        r0 = c * DN_CHUNK
        k = k_ref[0, r0:r0 + DN_CHUNK, :]
        v = v_ref[0, r0:r0 + DN_CHUNK, :]
        for d in range(2):
            beta = col_ref[0, 0, r0:r0 + DN_CHUNK, d:d + 1]
            gcol = col_ref[0, 0, r0:r0 + DN_CHUNK, 2 + d:3 + d]
            grow = row_ref[0, 0, c, d:d + 1, :]
            strict = masks[d][0]
            decay = jnp.exp(jnp.where(strict, gcol - grow, NEG_BIG))
            kb = k * beta
            lm = _dot(kb, k, NT_DIMS) * decay
            inv = _neumann_inverse(lm)
            u_ref[0, d, r0:r0 + DN_CHUNK, :] = _dot3(inv, v * beta)
            w_ref[0, d, r0:r0 + DN_CHUNK, :] = _dot3(inv, kb * jnp.exp(gcol))


def _dn_intra(qkv, cols, grows, nb, rows):
    m = nb * rows
    blk = DN_CG * DN_CHUNK
    nblk = rows // blk
    shape = jax.ShapeDtypeStruct((DN_HEADS, 2, m, LANES), F32)
    return pl.pallas_call(
        _dn_intra_kernel,
        grid=(nb, DN_HEADS, nblk),
        in_specs=[pl.BlockSpec((1, blk, LANES), lambda b, h, g: (DN_HEADS + h, b * nblk + g, 0)),
                  pl.BlockSpec((1, blk, LANES), lambda b, h, g: (2 * DN_HEADS + h, b * nblk + g, 0)),
                  pl.BlockSpec((1, 1, blk, 8), lambda b, h, g: (b, h, g, 0)),
                  pl.BlockSpec((1, 1, DN_CG, 2, DN_CHUNK), lambda b, h, g: (b, h, g, 0, 0))],
        out_specs=[pl.BlockSpec((1, 2, blk, LANES), lambda b, h, g: (h, 0, b * nblk + g, 0)),
                   pl.BlockSpec((1, 2, blk, LANES), lambda b, h, g: (h, 0, b * nblk + g, 0))],
        out_shape=[shape, shape],
        compiler_params=_cparams("arbitrary", "arbitrary", "arbitrary"),
        name="dn_intra",
    )(qkv, qkv, cols, grows)


def _dn_scan_kernel(q_ref, k_ref, u_ref, w_ref, col_ref, row_ref, z_ref, nw_ref, o_ref, acc_ref, *, n_chunks):
    masks = _chunk_masks(DN_CHUNK)
    acc_ref[...] = jnp.zeros_like(acc_ref)
    n_x = n_chunks[0]
    n_all = n_chunks[0] + n_chunks[1]

    def body(i, carry):
        states = []
        for d in range(2):
            s = carry[d]
            if d == 0:
                c = jnp.where(i < n_chunks[1], n_x + i, i - n_chunks[1])
            else:
                c = n_all - 1 - i
            r0 = pl.multiple_of(c * DN_CHUNK, DN_CHUNK)
            q = q_ref[0, pl.ds(r0, DN_CHUNK), :]
            k = k_ref[0, pl.ds(r0, DN_CHUNK), :]
            u = u_ref[0, d, pl.ds(r0, DN_CHUNK), :]
            w = w_ref[0, d, pl.ds(r0, DN_CHUNK), :]
            gcol = col_ref[0, 0, pl.ds(r0, DN_CHUNK), 2 + d:3 + d]
            grow = row_ref[0, 0, pl.ds(c, 1), d:d + 1, :][0]
            incl = masks[d][1]
            decay = jnp.exp(jnp.where(incl, gcol - grow, NEG_BIG))
            attn = _dot(q, k, NT_DIMS) * decay
            glast = gcol[DN_CHUNK - 1:DN_CHUNK, :] if d == 0 else gcol[0:1, :]
            qg = q * jnp.exp(gcol)
            kg = k * jnp.exp(glast - gcol)
            v_new = u - _dot(w, s)
            o = _dot(qg, s) + _dot(attn, v_new)
            acc_ref[pl.ds(r0, DN_CHUNK), :] += o
            states.append(s * jnp.exp(glast) + _dot(kg, v_new, TN_DIMS))
        return tuple(states)

    s0 = jnp.zeros((DN_HEAD_DIM, DN_HEAD_DIM), F32)
    lax.fori_loop(0, n_all, body, (s0, s0))
    o = acc_ref[...]
    y = o * lax.rsqrt(jnp.mean(o * o, axis=-1, keepdims=True) + NORM_EPS) * nw_ref[...]
    o_ref[0] = (y * _silu(z_ref[0].astype(F32))).astype(o_ref.dtype)


def _dn_scan(qkv, u, w, cols, grows, p, norm_w, nb, n_x, n_c, z_tile0):
    rows = n_x + n_c
    m = nb * rows
    n_chunks = (n_x // DN_CHUNK, n_c // DN_CHUNK)
    return pl.pallas_call(
        functools.partial(_dn_scan_kernel, n_chunks=n_chunks),
        grid=(nb, DN_HEADS),
        in_specs=[pl.BlockSpec((1, rows, LANES), lambda b, h: (h, b, 0)),
                  pl.BlockSpec((1, rows, LANES), lambda b, h: (DN_HEADS + h, b, 0)),
                  pl.BlockSpec((1, 2, rows, LANES), lambda b, h: (h, 0, b, 0)),
                  pl.BlockSpec((1, 2, rows, LANES), lambda b, h: (h, 0, b, 0)),
                  pl.BlockSpec((1, 1, rows, 8), lambda b, h: (b, h, 0, 0)),
                  pl.BlockSpec((1, 1, rows // DN_CHUNK, 2, DN_CHUNK), lambda b, h: (b, h, 0, 0, 0)),
                  pl.BlockSpec((1, rows, LANES), lambda b, h: (z_tile0 + h, b, 0)),
                  pl.BlockSpec((1, LANES), lambda b, h: (0, 0))],
        out_specs=pl.BlockSpec((1, rows, LANES), lambda b, h: (h, b, 0)),
        out_shape=jax.ShapeDtypeStruct((DN_HEADS, m, LANES), BF16),
        scratch_shapes=[pltpu.VMEM((rows, LANES), F32)],
        compiler_params=_cparams("arbitrary", "arbitrary"),
        name="dn_scan",
    )(qkv, qkv, u, w, cols, grows, p, norm_w.reshape(1, LANES).astype(F32))


def _deltanet_mixer(p, small, conv_w, a_log, dt_bias, norm_w, nb, n_x, n_c):
    rows = n_x + n_c
    qkv = _dn_conv(p, conv_w, nb, n_x, n_c)
    gates = _dn_gates(small, a_log, dt_bias).reshape(nb, rows, LANES)
    h = DN_HEADS
    cols = jnp.stack([gates[..., 0:h], gates[..., h:2 * h], gates[..., 4 * h:5 * h], gates[..., 5 * h:6 * h]],
                     axis=-1)
    cols = jnp.pad(cols.transpose(0, 2, 1, 3), ((0, 0), (0, 0), (0, 0), (0, 4)))
    grows = jnp.stack([gates[..., 4 * h:5 * h], gates[..., 5 * h:6 * h]], axis=-1)
    grows = grows.reshape(nb, rows // DN_CHUNK, DN_CHUNK, h, 2).transpose(0, 3, 1, 4, 2)
    u, w = _dn_intra(qkv, cols, grows, nb, rows)
    return _dn_scan(qkv, u, w, cols, grows, p, norm_w, nb, n_x, n_c, QKV_TILES)


def _gla_kernel(q_ref, k_ref, v_ref, r_ref, sm_ref, w2_ref, gb_ref, nw_ref, o_ref,
                gl_ref, acc_ref, st_ref, *, n_x, n_c):
    rows = n_x + n_c
    cs = GLA_CHUNK
    sm = sm_ref[...].astype(BF16)
    for d in range(2):
        z = jnp.dot(sm, w2_ref[d].astype(BF16), preferred_element_type=F32) + gb_ref[d]
        gl_ref[d] = (jnp.minimum(z, 0.0) - jnp.log1p(jnp.exp(-jnp.abs(z)))) * (1.0 / GLA_TAU)
    acc_ref[...] = jnp.zeros_like(acc_ref)
    st_ref[...] = jnp.zeros_like(st_ref)
    ri = lax.broadcasted_iota(jnp.int32, (cs, cs), 0)
    ci = lax.broadcasted_iota(jnp.int32, (cs, cs), 1)
    incl = (ri >= ci, ri <= ci)
    nxc, ncc = n_x // cs, n_c // cs

    def body(i, carry):
        for d in range(2):
            c = jnp.where(i < ncc, nxc + i, i - ncc) if d == 0 else nxc + ncc - 1 - i
            r0 = pl.multiple_of(c * cs, cs)
            g = gl_ref[d, pl.ds(r0, cs), :]
            cum = _tri_dot(incl[d].astype(F32), g)
            q = jnp.concatenate([q_ref[t, pl.ds(r0, cs), :] for t in range(2)], axis=1)
            k = jnp.concatenate([k_ref[t, pl.ds(r0, cs), :] for t in range(2)], axis=1)
            v = jnp.concatenate([v_ref[t, pl.ds(r0, cs), :] for t in range(4)], axis=1)
            q_in = q * jnp.exp(cum) * (GLA_DK ** -0.5)
            k_in = k * jnp.exp(-cum)
            attn = jnp.where(incl[d], _dot(q_in, k_in, NT_DIMS), 0.0)
            clast = cum[cs - 1:cs, :] if d == 0 else cum[0:1, :]
            k_dec = k * jnp.exp(clast - cum)
            st = st_ref[d]
            o = _dot(q_in, st, NT_DIMS) + _dot(attn, v)
            st_ref[d] = st * jnp.exp(clast) + _dot(v, k_dec, TN_DIMS)

            @pl.when(c < nxc)
            def _():
                acc_ref[pl.ds(r0, cs), :] += o
        return carry

    lax.fori_loop(0, rows // cs, body, 0)
    o = acc_ref[...]
    y = o * lax.rsqrt(jnp.mean(o * o, axis=-1, keepdims=True) + NORM_EPS) * nw_ref[...]
    for t in range(GLA_DV // LANES):
        r = r_ref[t, 0:n_x, :].astype(F32)
        o_ref[t] = (y[:, t * LANES:(t + 1) * LANES] * _silu(r)).astype(o_ref.dtype)


def _gla_mixer(p, small, gate_w2, gate_b, norm_w, nb, n_x, n_c):
    rows = n_x + n_c
    w2p = jnp.zeros((2, LANES, GLA_HEADS * GLA_DK), F32)
    for d in range(2):
        w2p = w2p.at[d, d * GLA_RANK:(d + 1) * GLA_RANK].set(gate_w2[d].astype(F32))
    qt, vt = GLA_DK // LANES, GLA_DV // LANES
    nq = GLA_HEADS
    return pl.pallas_call(
        functools.partial(_gla_kernel, n_x=n_x, n_c=n_c),
        grid=(nb, GLA_HEADS),
        in_specs=[pl.BlockSpec((qt, rows, LANES), lambda b, h: (h, b, 0), pipeline_mode=pl.Buffered(1)),
                  pl.BlockSpec((qt, rows, LANES), lambda b, h: (nq + h, b, 0), pipeline_mode=pl.Buffered(1)),
                  pl.BlockSpec((vt, rows, LANES), lambda b, h: (nq + h, b, 0), pipeline_mode=pl.Buffered(1)),
                  pl.BlockSpec((vt, rows, LANES), lambda b, h: (2 * nq + h, b, 0), pipeline_mode=pl.Buffered(1)),
                  pl.BlockSpec((rows, LANES), lambda b, h: (b, 0)),
                  pl.BlockSpec((2, LANES, GLA_DK), lambda b, h: (0, 0, h)),
                  pl.BlockSpec((2, 1, GLA_DK), lambda b, h: (0, 0, h)),
                  pl.BlockSpec((1, GLA_DV), lambda b, h: (0, 0))],
        out_specs=pl.BlockSpec((vt, n_x, LANES), lambda b, h: (h, b, 0)),
        out_shape=jax.ShapeDtypeStruct((GLA_HEADS * vt, nb * n_x, LANES), BF16),
        scratch_shapes=[pltpu.VMEM((2, rows, GLA_DK), F32),
                        pltpu.VMEM((n_x, GLA_DV), F32),
                        pltpu.VMEM((2, GLA_DV, GLA_DK), F32)],
        compiler_params=_cparams("arbitrary", "arbitrary"),
        name="gla",
    )(p, p, p, p, small, w2p, gate_b.astype(F32).reshape(2, 1, GLA_HEADS * GLA_DK),
      norm_w.astype(F32).reshape(1, GLA_DV))


def _router_kernel(h_ref, w_ref, e_ref, g_ref):
    logits = _dot3(h_ref[...], w_ref[...])
    lane = lax.broadcasted_iota(jnp.int32, logits.shape, 1).astype(F32)
    logits = jnp.where(lane < N_EXPERTS, logits, NEG_BIG)
    m1 = jnp.max(logits, axis=-1, keepdims=True)
    i1 = jnp.min(jnp.where(logits == m1, lane, float(LANES)), axis=-1, keepdims=True)
    rest = jnp.where(lane == i1, NEG_BIG, logits)
    m2 = jnp.max(rest, axis=-1, keepdims=True)
    i2 = jnp.min(jnp.where(rest == m2, lane, float(LANES)), axis=-1, keepdims=True)
    e2 = jnp.exp(m2 - m1)
    den = 1.0 + e2
    e_ref[...] = jnp.where(lane == 0.0, i1, jnp.where(lane == 1.0, i2, 0.0)).astype(jnp.int32)
    g_ref[...] = jnp.where(lane == 0.0, 1.0 / den, jnp.where(lane == 1.0, e2 / den, 0.0))


def _router(h, w_router):
    m = h.shape[0]
    tm = 512
    wp = jnp.pad(w_router.astype(F32), ((0, 0), (0, LANES - N_EXPERTS)))
    return pl.pallas_call(
        _router_kernel,
        grid=(m // tm,),
        in_specs=[pl.BlockSpec((tm, D_MODEL), lambda i: (i, 0)),
                  pl.BlockSpec((D_MODEL, LANES), lambda i: (0, 0))],
        out_specs=[pl.BlockSpec((tm, LANES), lambda i: (i, 0)),
                   pl.BlockSpec((tm, LANES), lambda i: (i, 0))],
        out_shape=[jax.ShapeDtypeStruct((m, LANES), jnp.int32),
                   jax.ShapeDtypeStruct((m, LANES), F32)],
        compiler_params=_cparams("arbitrary"),
        name="router",
    )(h, wp)


def _row_copy(src_ref, src_row, dst_ref, dst_row, sem):
    return pltpu.make_async_copy(src_ref.at[pl.ds(src_row, 1)], dst_ref.at[pl.ds(dst_row, 1)], sem)


def _moe_kernel(be_ref, nused_ref, tok_ref, dst_ref, h_ref, w1_ref, w3_ref, w2_ref, y_ref,
                a32_ref, a16_ref, acc_ref, sem):
    i = pl.program_id(0)
    j = pl.program_id(1)
    nj = pl.num_programs(1)
    base = i * MOE_TM

    @pl.when(i < nused_ref[0])
    def _():
        @pl.when(j == 0)
        def _():
            def start(r, c):
                _row_copy(h_ref, tok_ref[base + r], a32_ref, r, sem).start()
                return c

            def wait(r, c):
                _row_copy(h_ref, tok_ref[base + r], a32_ref, r, sem).wait()
                return c

            lax.fori_loop(0, MOE_TM, start, 0)
            lax.fori_loop(0, MOE_TM, wait, 0)
            a16_ref[...] = a32_ref[...].astype(BF16)

        a = a16_ref[...]
        h1 = jnp.dot(a, w1_ref[0].astype(BF16), preferred_element_type=F32)
        h3 = jnp.dot(a, w3_ref[0].astype(BF16), preferred_element_type=F32)
        hh = (_silu(h1) * h3).astype(BF16)
        part = jnp.dot(hh, w2_ref[0].astype(BF16), preferred_element_type=F32)

        @pl.when(j == 0)
        def _():
            acc_ref[...] = part

        @pl.when(j > 0)
        def _():
            acc_ref[...] += part

        @pl.when(j == nj - 1)
        def _():
            def start(r, c):
                dst = dst_ref[base + r]

                @pl.when(dst >= 0)
                def _():
                    _row_copy(acc_ref, r, y_ref, dst, sem).start()
                return c

            def wait(r, c):
                dst = dst_ref[base + r]

                @pl.when(dst >= 0)
                def _():
                    _row_copy(acc_ref, r, y_ref, dst, sem).wait()
                return c

            lax.fori_loop(0, MOE_TM, start, 0)
            lax.fori_loop(0, MOE_TM, wait, 0)


def _moe(h, experts, w1, w3, w2):
    n_tok = h.shape[0]
    n_asg = n_tok * TOP_K
    n_blocks = -(-n_asg // MOE_TM) + N_EXPERTS
    n_rows = n_blocks * MOE_TM
    th = 256
    nj = EXPERT_HIDDEN // th
    flat_e = experts.reshape(n_asg)
    onehot = (flat_e[:, None] == jnp.arange(N_EXPERTS, dtype=jnp.int32)[None, :]).astype(jnp.int32)
    csum = jnp.cumsum(onehot, axis=0)
    rank = jnp.sum(csum * onehot, axis=1) - 1
    counts = csum[-1]
    padded = (counts + MOE_TM - 1) // MOE_TM * MOE_TM
    padded_end = jnp.cumsum(padded)
    padded_start = padded_end - padded
    dest = padded_start[flat_e] + rank
    asg = jnp.arange(n_asg, dtype=jnp.int32)
    row_tok = jnp.zeros((n_rows,), jnp.int32).at[dest].set(asg // TOP_K)
    row_dst = jnp.full((n_rows,), -1, jnp.int32).at[dest].set(asg)
    blk_start = jnp.arange(n_blocks, dtype=jnp.int32) * MOE_TM
    n_used = (padded_end[-1] // MOE_TM).astype(jnp.int32)
    blk_e = jnp.sum((padded_end[None, :] <= blk_start[:, None]).astype(jnp.int32), axis=1)
    last_e = jnp.sum((padded_end <= (n_used - 1) * MOE_TM).astype(jnp.int32))
    blk_e = jnp.where(blk_start < padded_end[-1], blk_e, last_e).astype(jnp.int32)

    def w_col(i, j, be, nu, *_):
        return jnp.where(i < nu[0], j, nj - 1)

    grid_spec = pltpu.PrefetchScalarGridSpec(
        num_scalar_prefetch=4,
        grid=(n_blocks, nj),
        in_specs=[pl.BlockSpec(memory_space=pl.ANY),
                  pl.BlockSpec((1, D_MODEL, th), lambda i, j, be, *s: (be[i], 0, w_col(i, j, be, *s))),
                  pl.BlockSpec((1, D_MODEL, th), lambda i, j, be, *s: (be[i], 0, w_col(i, j, be, *s))),
                  pl.BlockSpec((1, th, D_MODEL), lambda i, j, be, *s: (be[i], w_col(i, j, be, *s), 0))],
        out_specs=pl.BlockSpec(memory_space=pl.ANY),
        scratch_shapes=[pltpu.VMEM((MOE_TM, D_MODEL), F32),
                        pltpu.VMEM((MOE_TM, D_MODEL), BF16),
                        pltpu.VMEM((MOE_TM, D_MODEL), F32),
                        pltpu.SemaphoreType.DMA(())],
    )
    return pl.pallas_call(
        _moe_kernel,
        grid_spec=grid_spec,
        out_shape=jax.ShapeDtypeStruct((n_asg, D_MODEL), F32),
        compiler_params=_cparams("arbitrary", "arbitrary"),
        name="moe",
    )(blk_e, n_used.reshape(1), row_tok, row_dst, h, w1, w3, w2)


def kernel(x, c, ctx, c_ctx, ada_w, ada_b, ln1_g, ln1_b, ln2_g, ln2_b, e_w_in, s5_lam_re, s5_lam_im, s5_log_step, s5_b_re, s5_b_im, s5_c_re, s5_c_im, s5_d, dn_conv, dn_a_log, dn_dt_bias, dn_norm_w, e_w_out, ffn_w1, ffn_w3, ffn_w2, o_w_in, gla_w2, gla_b, gla_norm_w, o_w_out, moe_router, moe_w1, moe_w3, moe_w2):
    nb, n_x, _ = x.shape
    n_c = ctx.shape[1]
    rows = n_x + n_c
    m_all = nb * rows
    assert nb < 8 and n_x % (GRID_W * 8) == 0 and n_c % ROW_BLK == 0 and n_x % ROW_BLK == 0

    c8 = jnp.zeros((8, D_MODEL), F32).at[:nb].set(c).at[nb].set(c_ctx)
    mods = _adaln(c8, ada_w, ada_b).reshape(DEPTH, 8, N_MOD, D_MODEL)
    xc = jnp.concatenate([x, ctx], axis=1)

    h = _modulate(xc, mods, 0, n_x)
    p, small = _mm_in(h.reshape(m_all, D_MODEL), e_w_in[0])
    tables = _s5_tables(s5_lam_re[0].astype(F32), s5_lam_im[0].astype(F32), s5_log_step[0].astype(F32),
                        s5_b_re[0].astype(F32), s5_b_im[0].astype(F32), s5_c_re[0].astype(F32),
                        s5_c_im[0].astype(F32), s5_d[0])
    u_tile0 = (3 * DN_WIDTH + DN_WIDTH) // LANES
    y_s5 = _s5_mixer(p, tables, nb, n_x // S5_L, n_c // S5_L, u_tile0, u_tile0 + S5_TILES)
    y_dn = _deltanet_mixer(p, small, dn_conv[0], dn_a_log[0], dn_dt_bias[0], dn_norm_w[0], nb, n_x, n_c)
    mix = _mm_out([y_s5, y_dn], e_w_out[0]).reshape(nb, rows, D_MODEL)
    xc, h = _resid_ln(xc, mix, mods, 0, 2, 0, 3, ln1_g[0], ln1_b[0], n_x, rows, h_dtype=BF16)
    f = _ffn(h.reshape(m_all, D_MODEL), ffn_w1[0], ffn_w3[0], ffn_w2[0]).reshape(nb, rows, D_MODEL)
    xc, h = _resid_ln(xc, f, mods, 0, 5, 1, 0, ln2_g[0], ln2_b[0], n_x, rows, h_dtype=BF16)

    p, small = _mm_in(h.reshape(m_all, D_MODEL), o_w_in[0])
    y_gla = _gla_mixer(p, small, gla_w2[0], gla_b[0], gla_norm_w[0], nb, n_x, n_c)
    mix = _mm_out([y_gla], o_w_out[0]).reshape(nb, n_x, D_MODEL)
    xl, h = _resid_ln(xc, mix, mods, 1, 2, 1, 3, ln1_g[1], ln1_b[1], n_x, n_x, h_dtype=F32)
    hf = h.reshape(nb * n_x, D_MODEL)
    experts, gates = _router(hf, moe_router[0])
    y = _moe(hf, experts[:, :TOP_K], moe_w1[0], moe_w3[0], moe_w2[0])
    out, _ = _resid_ln(xl, y.reshape(nb, n_x, TOP_K * D_MODEL), mods, 1, 5, 1, 0, ln2_g[1], ln2_b[1], n_x, n_x,
                       moe_gates=gates[:, :TOP_K].reshape(nb, n_x, TOP_K))
    return out
```

```python
import functools
import math

import jax
import jax.numpy as jnp
from jax import lax
from jax.experimental import pallas as pl
from jax.experimental.pallas import tpu as pltpu

F32 = jnp.float32
BF16 = jnp.bfloat16

LANES = 128
VMEM_LIMIT = 60 * 1024 * 1024

D_MODEL = 2048
DEPTH = 2
ALPHA = (2 * DEPTH) ** 0.25
NORM_EPS = 1e-6
N_MOD = 6
GRID_W = 64

S5_WIDTH = D_MODEL // 2
S5_GROUP = 16
S5_GROUPS = S5_WIDTH // S5_GROUP
S5_STATE = 64
S5_L = 16
S5_TILES = S5_WIDTH // LANES
S5_GPT = LANES // S5_GROUP
S5_SW = S5_GPT * S5_STATE

DN_WIDTH = D_MODEL // 2
DN_HEAD_DIM = 128
DN_HEADS = DN_WIDTH // DN_HEAD_DIM
DN_CHUNK = 64
QKV_TILES = 3 * DN_WIDTH // LANES

GLA_HEADS = 4
GLA_DK = 256
GLA_DV = 512
GLA_RANK = 16
GLA_TAU = 16.0
GLA_CHUNK = 64
GLA_SUB = 16

MAIN_COLS = 6144
FFN_HIDDEN = 5632
N_EXPERTS = 8
TOP_K = 2
EXPERT_HIDDEN = 7168
MOE_TM = 1024
ROW_BLK = 256
NEG_BIG = -1e30

NT_DIMS = (((1,), (1,)), ((), ()))
TN_DIMS = (((0,), (0,)), ((), ()))


def _cparams(*sem):
    return pltpu.CompilerParams(dimension_semantics=sem, vmem_limit_bytes=VMEM_LIMIT)


def _dot(a, b, dims=None):
    a = a.astype(BF16)
    b = b.astype(BF16)
    if dims is None:
        return jnp.dot(a, b, preferred_element_type=F32)
    return lax.dot_general(a, b, dims, preferred_element_type=F32)


def _tri_dot(tri, g):
    g1 = g.astype(BF16)
    r1 = g - g1.astype(F32)
    g2 = r1.astype(BF16)
    g3 = (r1 - g2.astype(F32)).astype(BF16)
    t = tri.astype(BF16)
    return (jnp.dot(t, g1, preferred_element_type=F32) + jnp.dot(t, g2, preferred_element_type=F32)
            + jnp.dot(t, g3, preferred_element_type=F32))


def _silu(x):
    return x * jax.nn.sigmoid(x)


def _softplus(x):
    return jnp.maximum(x, 0.0) + jnp.log1p(jnp.exp(-jnp.abs(x)))


def _adaln_kernel(c_ref, w_ref, b_ref, o_ref):
    o_ref[0] = _dot(_silu(c_ref[...]), w_ref[0]) + b_ref[0]


def _adaln(c8, ada_w, ada_b):
    tn = 1024
    n = N_MOD * D_MODEL
    return pl.pallas_call(
        _adaln_kernel,
        grid=(DEPTH, n // tn),
        in_specs=[pl.BlockSpec((8, D_MODEL), lambda l, j: (0, 0)),
                  pl.BlockSpec((1, D_MODEL, tn), lambda l, j: (l, 0, j)),
                  pl.BlockSpec((1, 1, tn), lambda l, j: (l, 0, j))],
        out_specs=pl.BlockSpec((1, 8, tn), lambda l, j: (l, 0, j)),
        out_shape=jax.ShapeDtypeStruct((DEPTH, 8, n), F32),
        compiler_params=_cparams("arbitrary", "arbitrary"),
        name="adaln",
    )(c8, ada_w, ada_b.reshape(DEPTH, 1, n))


def _mod_row(nb, nx_blocks):
    return lambda b, r: jnp.where(r < nx_blocks, b, nb)


def _mod_kernel(x_ref, m_ref, o_ref):
    sh = m_ref[0, 0, 0:1, :]
    sc = m_ref[0, 0, 1:2, :]
    o_ref[0] = (x_ref[0] * (1.0 + sc) + sh).astype(o_ref.dtype)


def _modulate(xc, mods, layer, nx):
    nb, rows, _ = xc.shape
    row = _mod_row(nb, nx // ROW_BLK)
    return pl.pallas_call(
        _mod_kernel,
        grid=(nb, rows // ROW_BLK),
        in_specs=[pl.BlockSpec((1, ROW_BLK, D_MODEL), lambda b, r: (b, r, 0)),
                  pl.BlockSpec((1, 1, N_MOD, D_MODEL), lambda b, r: (layer, row(b, r), 0, 0))],
        out_specs=pl.BlockSpec((1, ROW_BLK, D_MODEL), lambda b, r: (b, r, 0)),
        out_shape=jax.ShapeDtypeStruct(xc.shape, BF16),
        compiler_params=_cparams("arbitrary", "arbitrary"),
        name="modulate",
    )(xc, mods)


def _ln_kernel(*refs, gate_row, moe, emit_h):
    it = iter(refs)
    x_ref, y_ref = next(it), next(it)
    gw_ref = next(it) if moe else None
    mg_ref, mn_ref, g_ref, b_ref, xo_ref = next(it), next(it), next(it), next(it), next(it)
    x = x_ref[0]
    if moe:
        gw = gw_ref[0]
        y = gw[:, 0:1] * y_ref[0, :, 0:D_MODEL] + gw[:, 1:2] * y_ref[0, :, D_MODEL:2 * D_MODEL]
    else:
        y = y_ref[0].astype(F32)
    v = ALPHA * x + mg_ref[0, 0, gate_row:gate_row + 1, :] * y
    mu = jnp.mean(v, axis=-1, keepdims=True)
    vc = v - mu
    var = jnp.mean(vc * vc, axis=-1, keepdims=True)
    xn = vc * lax.rsqrt(var + NORM_EPS) * g_ref[...] + b_ref[...]
    xo_ref[0] = xn
    if emit_h:
        ho_ref = next(it)
        sh = mn_ref[0, 0, 0:1, :]
        sc = mn_ref[0, 0, 1:2, :]
        ho_ref[0] = (xn * (1.0 + sc) + sh).astype(ho_ref.dtype)


def _resid_ln(xc, y, mods, gate_layer, gate_row, next_layer, next_row0, ln_g, ln_b, nx, rows_out,
              h_dtype=None, moe_gates=None):
    nb = xc.shape[0]
    row = _mod_row(nb, nx // ROW_BLK)
    moe = moe_gates is not None
    emit_h = h_dtype is not None
    mods_next = mods[:, :, next_row0:next_row0 + 2, :] if emit_h else mods[:, :, 0:2, :]
    in_specs = [pl.BlockSpec((1, ROW_BLK, D_MODEL), lambda b, r: (b, r, 0)),
                pl.BlockSpec((1, ROW_BLK, y.shape[-1]), lambda b, r: (b, r, 0))]
    args = [xc, y]
    if moe:
        in_specs.append(pl.BlockSpec((1, ROW_BLK, TOP_K), lambda b, r: (b, r, 0)))
        args.append(moe_gates)
    in_specs += [pl.BlockSpec((1, 1, N_MOD, D_MODEL), lambda b, r: (gate_layer, row(b, r), 0, 0)),
                 pl.BlockSpec((1, 1, 2, D_MODEL), lambda b, r: (next_layer, row(b, r), 0, 0)),
                 pl.BlockSpec((1, D_MODEL), lambda b, r: (0, 0)),
                 pl.BlockSpec((1, D_MODEL), lambda b, r: (0, 0))]
    args += [mods, mods_next, ln_g.reshape(1, D_MODEL), ln_b.reshape(1, D_MODEL)]
    out_shape = [jax.ShapeDtypeStruct((nb, rows_out, D_MODEL), F32)]
    out_specs = [pl.BlockSpec((1, ROW_BLK, D_MODEL), lambda b, r: (b, r, 0))]
    if emit_h:
        out_shape.append(jax.ShapeDtypeStruct((nb, rows_out, D_MODEL), h_dtype))
        out_specs.append(pl.BlockSpec((1, ROW_BLK, D_MODEL), lambda b, r: (b, r, 0)))
    res = pl.pallas_call(
        functools.partial(_ln_kernel, gate_row=gate_row, moe=moe, emit_h=emit_h),
        grid=(nb, rows_out // ROW_BLK),
        in_specs=in_specs, out_specs=out_specs, out_shape=out_shape,
        compiler_params=_cparams("arbitrary", "arbitrary"),
        name="resid_ln",
    )(*args)
    return res if emit_h else (res[0], None)


def _mm_in_kernel(a_ref, w_ref, ws_ref, o_ref, os_ref, *, tn):
    a = a_ref[...]
    res = jnp.dot(a, w_ref[...].astype(BF16), preferred_element_type=F32)
    for t in range(tn // LANES):
        o_ref[t] = res[:, t * LANES:(t + 1) * LANES].astype(o_ref.dtype)

    @pl.when(pl.program_id(1) == 0)
    def _():
        os_ref[...] = jnp.dot(a, ws_ref[...].astype(BF16), preferred_element_type=F32)


def _row_tile(m, cap):
    best = 16
    for t in range(16, cap + 1, 16):
        if m % t == 0:
            best = t
    return best


def _mm_in(a, w):
    m = a.shape[0]
    tm = _row_tile(m, 1088)
    tn = 512
    w_small = jnp.pad(w[:, MAIN_COLS:], ((0, 0), (0, LANES - (w.shape[1] - MAIN_COLS))))
    return pl.pallas_call(
        functools.partial(_mm_in_kernel, tn=tn),
        grid=(m // tm, MAIN_COLS // tn),
        in_specs=[pl.BlockSpec((tm, D_MODEL), lambda i, j: (i, 0)),
                  pl.BlockSpec((D_MODEL, tn), lambda i, j: (0, j)),
                  pl.BlockSpec((D_MODEL, LANES), lambda i, j: (0, 0))],
        out_specs=[pl.BlockSpec((tn // LANES, tm, LANES), lambda i, j: (j, i, 0)),
                   pl.BlockSpec((tm, LANES), lambda i, j: (i, 0))],
        out_shape=[jax.ShapeDtypeStruct((MAIN_COLS // LANES, m, LANES), BF16),
                   jax.ShapeDtypeStruct((m, LANES), F32)],
        compiler_params=_cparams("arbitrary", "arbitrary"),
        name="mm_in",
    )(a, w, w_small)


def _mm_out_kernel(*refs, n_a):
    a_refs, w_ref, o_ref = refs[:n_a], refs[n_a], refs[n_a + 1]
    a = jnp.concatenate([r[t] for r in a_refs for t in range(r.shape[0])], axis=1)
    o_ref[...] = jnp.dot(a, w_ref[...].astype(BF16), preferred_element_type=F32)


def _mm_out(a_list, w):
    m = a_list[0].shape[1]
    tm = _row_tile(m, 1088)
    tn = 512
    in_specs = [pl.BlockSpec((a.shape[0], tm, LANES), lambda i, j: (0, i, 0)) for a in a_list]
    in_specs.append(pl.BlockSpec((D_MODEL, tn), lambda i, j: (0, j)))
    return pl.pallas_call(
        functools.partial(_mm_out_kernel, n_a=len(a_list)),
        grid=(m // tm, D_MODEL // tn),
        in_specs=in_specs,
        out_specs=pl.BlockSpec((tm, tn), lambda i, j: (i, j)),
        out_shape=jax.ShapeDtypeStruct((m, D_MODEL), F32),
        compiler_params=_cparams("arbitrary", "arbitrary"),
        name="mm_out",
    )(*a_list, w)


def _ffn_kernel(a_ref, w1_ref, w3_ref, w2_ref, o_ref):
    j = pl.program_id(1)
    a = a_ref[...]
    h1 = jnp.dot(a, w1_ref[...].astype(BF16), preferred_element_type=F32)
    h3 = jnp.dot(a, w3_ref[...].astype(BF16), preferred_element_type=F32)
    hh = (_silu(h1) * h3).astype(BF16)
    part = jnp.dot(hh, w2_ref[...].astype(BF16), preferred_element_type=F32)

    @pl.when(j == 0)
    def _():
        o_ref[...] = part

    @pl.when(j > 0)
    def _():
        o_ref[...] += part


def _ffn(a, w1, w3, w2):
    m = a.shape[0]
    hidden = w1.shape[1]
    tm = _row_tile(m, 1088)
    th = 256
    return pl.pallas_call(
        _ffn_kernel,
        grid=(m // tm, hidden // th),
        in_specs=[pl.BlockSpec((tm, D_MODEL), lambda i, j: (i, 0)),
                  pl.BlockSpec((D_MODEL, th), lambda i, j: (0, j)),
                  pl.BlockSpec((D_MODEL, th), lambda i, j: (0, j)),
                  pl.BlockSpec((th, D_MODEL), lambda i, j: (j, 0))],
        out_specs=pl.BlockSpec((tm, D_MODEL), lambda i, j: (i, 0)),
        out_shape=jax.ShapeDtypeStruct((m, D_MODEL), F32),
        compiler_params=_cparams("arbitrary", "arbitrary"),
        name="ffn",
    )(a, w1, w3, w2)


def _s5_tables(lam_re, lam_im, log_step, b_re, b_im, c_re, c_im, d_skip):
    hp = lax.Precision.HIGHEST
    tau = jnp.arange(S5_L + 1, dtype=F32)[:, None, None]
    lag_k, ab, ca, a_l = [], [], [], []
    for d in range(2):
        delta = jnp.exp(log_step[d])[:, None]
        lr, li = lam_re[d], lam_im[d]
        mag = jnp.exp(lr * delta * tau)
        ang = li * delta * tau
        are, aim = mag * jnp.cos(ang), mag * jnp.sin(ang)
        den = jnp.square(lr) + jnp.square(li)
        f_re = ((are[1] - 1.0) * lr + aim[1] * li) / den
        f_im = (aim[1] * lr - (are[1] - 1.0) * li) / den
        bbr = f_re[..., None] * b_re[d] - f_im[..., None] * b_im[d]
        bbi = f_re[..., None] * b_im[d] + f_im[..., None] * b_re[d]
        car = c_re[d][None] * are[:, :, None, :] - c_im[d][None] * aim[:, :, None, :]
        cai = c_re[d][None] * aim[:, :, None, :] + c_im[d][None] * are[:, :, None, :]
        lag_k.append(jnp.einsum('tghp,gpk->tghk', car, bbr, precision=hp)
                     - jnp.einsum('tghp,gpk->tghk', cai, bbi, precision=hp))
        ab.append((are[..., None] * bbr[None] - aim[..., None] * bbi[None],
                   are[..., None] * bbi[None] + aim[..., None] * bbr[None]))
        ca.append((car, cai))
        a_l.append((are[S5_L], aim[S5_L]))
    eye = jnp.eye(S5_GPT, dtype=F32)
    kf, kb = lag_k
    lag = jnp.concatenate([kb[S5_L - 1:0:-1], (kf[0] + kb[0])[None], kf[1:S5_L]], axis=0)
    lag = lag.reshape(2 * S5_L - 1, S5_TILES, S5_GPT, S5_GROUP, S5_GROUP)
    wlag = jnp.einsum('ljghk,gf->jgklfh', lag, eye).reshape(S5_TILES, LANES, (2 * S5_L - 1) * LANES)
    wb = jnp.stack([ab[0][0][:S5_L][::-1], ab[0][1][:S5_L][::-1], ab[1][0][:S5_L], ab[1][1][:S5_L]], axis=0)
    wb = wb.reshape(4, S5_L, S5_TILES, S5_GPT, S5_STATE, S5_GROUP)
    wb = jnp.einsum('csjgpk,gf->jsgkcfp', wb, eye).reshape(S5_TILES, S5_L * LANES, 4 * S5_SW)
    wc = jnp.stack([ca[0][0][1:], -ca[0][1][1:], ca[1][0][S5_L:0:-1], -ca[1][1][S5_L:0:-1]], axis=0)
    wc = wc.reshape(4, S5_L, S5_TILES, S5_GPT, S5_GROUP, S5_STATE)
    wc = jnp.einsum('ctjghp,gf->jcfptgh', wc, eye).reshape(S5_TILES, 4 * S5_SW, S5_L * LANES)
    a16 = jnp.stack([a_l[0][0], a_l[0][1], a_l[1][0], a_l[1][1]], axis=0)
    a16 = a16.reshape(4, S5_TILES, S5_SW).transpose(1, 0, 2)
    dsk = jnp.tile(d_skip.astype(F32).reshape(S5_TILES, 1, LANES), (1, 1, S5_L))
    return wlag.astype(BF16), wb.astype(BF16), wc.astype(BF16), a16, dsk


def _s5_kernel(u_ref, g_ref, wlag_ref, wb_ref, wc_ref, a16_ref, d_ref, o_ref,
               wt_ref, s_ref, hin_ref, *, n_x, n_c):
    rows = n_x + n_c

    @pl.when(pl.program_id(1) == 0)
    def _():
        for s in range(S5_L):
            wt_ref[s * LANES:(s + 1) * LANES, :] = wlag_ref[0, :, (S5_L - 1 - s) * LANES:(2 * S5_L - 1 - s) * LANES]

    ub = u_ref[0]
    u = ub.astype(F32)
    s_ref[...] = jnp.dot(ub, wb_ref[0], preferred_element_type=F32)
    a = a16_ref[0]
    afr, afi, abr, abi = a[0:1], a[1:2], a[2:3], a[3:4]
    w = S5_SW

    sub = 8
    tiles_x, tiles_c = n_x // sub, n_c // sub

    def tile_scan(t, hr, hi, ar, ai, lane0, forward):
        r0 = pl.multiple_of(t * sub, sub)
        sr = s_ref[pl.ds(r0, sub), lane0:lane0 + w]
        si = s_ref[pl.ds(r0, sub), lane0 + w:lane0 + 2 * w]
        in_r, in_i = [None] * sub, [None] * sub
        for r in (range(sub) if forward else range(sub - 1, -1, -1)):
            in_r[r], in_i[r] = hr, hi
            hr, hi = ar * hr - ai * hi + sr[r:r + 1], ar * hi + ai * hr + si[r:r + 1]
        hin_ref[pl.ds(r0, sub), lane0:lane0 + w] = jnp.concatenate(in_r, axis=0)
        hin_ref[pl.ds(r0, sub), lane0 + w:lane0 + 2 * w] = jnp.concatenate(in_i, axis=0)
        return hr, hi

    def body(i, carry):
        hfr, hfi, hbr, hbi = carry
        tf = jnp.where(i < tiles_c, tiles_x + i, i - tiles_c)
        tb = tiles_x + tiles_c - 1 - i
        hfr, hfi = tile_scan(tf, hfr, hfi, afr, afi, 0, True)
        hbr, hbi = tile_scan(tb, hbr, hbi, abr, abi, 2 * w, False)
        return hfr, hfi, hbr, hbi

    z = jnp.zeros((1, w), F32)
    lax.fori_loop(0, tiles_x + tiles_c, body, (z, z, z, z))
    y = (jnp.dot(ub, wt_ref[...], preferred_element_type=F32)
         + jnp.dot(hin_ref[...].astype(BF16), wc_ref[0], preferred_element_type=F32))
    y = y + d_ref[0] * u
    gelu = 0.5 * y * (1.0 + jnp.tanh(math.sqrt(2.0 / math.pi) * (y + 0.044715 * (y * y * y))))
    o_ref[0] = (gelu * jax.nn.sigmoid(g_ref[0].astype(F32))).astype(o_ref.dtype)


def _s5_mixer(p, tables, nb, n_x, n_c, u_tile0, g_tile0):
    wlag, wb, wc, a16, dsk = tables
    rows = n_x + n_c
    width = S5_L * LANES
    pv = p.reshape(p.shape[0], nb * rows, width)
    return pl.pallas_call(
        functools.partial(_s5_kernel, n_x=n_x, n_c=n_c),
        grid=(S5_TILES, nb),
        in_specs=[pl.BlockSpec((1, rows, width), lambda j, b: (u_tile0 + j, b, 0)),
                  pl.BlockSpec((1, rows, width), lambda j, b: (g_tile0 + j, b, 0)),
                  pl.BlockSpec((1, LANES, wlag.shape[2]), lambda j, b: (j, 0, 0), pipeline_mode=pl.Buffered(1)),
                  pl.BlockSpec((1, width, 4 * S5_SW), lambda j, b: (j, 0, 0), pipeline_mode=pl.Buffered(1)),
                  pl.BlockSpec((1, 4 * S5_SW, width), lambda j, b: (j, 0, 0), pipeline_mode=pl.Buffered(1)),
                  pl.BlockSpec((1, 4, S5_SW), lambda j, b: (j, 0, 0)),
                  pl.BlockSpec((1, 1, width), lambda j, b: (j, 0, 0))],
        out_specs=pl.BlockSpec((1, rows, width), lambda j, b: (j, b, 0)),
        out_shape=jax.ShapeDtypeStruct((S5_TILES, nb * rows, width), BF16),
        scratch_shapes=[pltpu.VMEM((width, width), BF16),
                        pltpu.VMEM((rows, 4 * S5_SW), F32),
                        pltpu.VMEM((rows, 4 * S5_SW), F32)],
        compiler_params=_cparams("arbitrary", "arbitrary"),
        name="s5",
    )(pv, pv, wlag, wb, wc, a16, dsk).reshape(S5_TILES, nb * rows * S5_L, LANES)


def _dn_conv_kernel(x_ref, w_ref, o_ref, xp_ref, cp_ref, *, n_x, n_c):
    pad = 72
    blk = 512
    tile = pl.program_id(0)
    w = w_ref[...]
    xp_ref[0:pad, :] = jnp.zeros((pad, LANES), F32)
    xp_ref[pad + n_x:pad + n_x + pad, :] = jnp.zeros((pad, LANES), F32)
    xp_ref[pad:pad + n_x, :] = x_ref[0, 0:n_x, :].astype(F32)
    cp_ref[0:8, :] = jnp.zeros((8, LANES), F32)
    cp_ref[8 + n_c:16 + n_c, :] = jnp.zeros((8, LANES), F32)
    cp_ref[8:8 + n_c, :] = x_ref[0, n_x:n_x + n_c, :].astype(F32)
    is_qk = tile < 2 * DN_HEADS
    scale = jnp.where(tile < DN_HEADS, DN_HEAD_DIM ** -0.5, 1.0)

    def finish(y):
        y = _silu(y)
        nrm = y * lax.rsqrt(jnp.sum(y * y, axis=-1, keepdims=True) + NORM_EPS) * scale
        return jnp.where(is_qk, nrm, y)

    col = lax.broadcasted_iota(jnp.int32, (blk, LANES), 0) & (GRID_W - 1)
    not_first = col != 0
    not_last = col != GRID_W - 1
    for r0 in range(0, n_x, blk):
        left = mid = right = None
        for kh in range(3):
            base = pad + r0 + (kh - 1) * GRID_W
            tl = xp_ref[base - 1:base - 1 + blk, :] * w[3 * kh:3 * kh + 1, :]
            tm = xp_ref[base:base + blk, :] * w[3 * kh + 1:3 * kh + 2, :]
            tr = xp_ref[base + 1:base + 1 + blk, :] * w[3 * kh + 2:3 * kh + 3, :]
            left = tl if left is None else left + tl
            mid = tm if mid is None else mid + tm
            right = tr if right is None else right + tr
        y = jnp.where(not_first, left, 0.0) + mid + jnp.where(not_last, right, 0.0)
        o_ref[0, r0:r0 + blk, :] = finish(y)
    yc = (cp_ref[7:7 + n_c, :] * w[3:4, :] + cp_ref[8:8 + n_c, :] * w[4:5, :]
          + cp_ref[9:9 + n_c, :] * w[5:6, :])
    o_ref[0, n_x:n_x + n_c, :] = finish(yc)


def _dn_conv(p, conv_w, nb, n_x, n_c):
    rows = n_x + n_c
    w9 = conv_w.reshape(9, 3 * DN_WIDTH)
    return pl.pallas_call(
        functools.partial(_dn_conv_kernel, n_x=n_x, n_c=n_c),
        grid=(QKV_TILES, nb),
        in_specs=[pl.BlockSpec((1, rows, LANES), lambda t, b: (t, b, 0)),
                  pl.BlockSpec((9, LANES), lambda t, b: (0, t))],
        out_specs=pl.BlockSpec((1, rows, LANES), lambda t, b: (t, b, 0)),
        out_shape=jax.ShapeDtypeStruct((QKV_TILES, nb * rows, LANES), F32),
        scratch_shapes=[pltpu.VMEM((n_x + 144, LANES), F32), pltpu.VMEM((n_c + 16, LANES), F32)],
        compiler_params=_cparams("arbitrary", "arbitrary"),
        name="dn_conv",
    )(p, w9)


def _dn_gates_kernel(s_ref, coef_ref, dtb_ref, o_ref, *, tm):
    s = s_ref[...]
    lane = lax.broadcasted_iota(jnp.int32, (tm, LANES), 1)
    beta = jax.nn.sigmoid(s)
    g = coef_ref[...] * _softplus(s + dtb_ref[...])
    o_ref[...] = jnp.where(lane < 2 * DN_HEADS, beta, g)
    ri = lax.broadcasted_iota(jnp.int32, (DN_CHUNK, DN_CHUNK), 0)
    ci = lax.broadcasted_iota(jnp.int32, (DN_CHUNK, DN_CHUNK), 1)
    lower = (ri >= ci).astype(F32)
    upper = (ri <= ci).astype(F32)
    lane_c = lax.broadcasted_iota(jnp.int32, (DN_CHUNK, LANES), 1)
    for c0 in range(0, tm, DN_CHUNK):
        gc = g[c0:c0 + DN_CHUNK, :]
        gsh = pltpu.roll(gc, 2 * DN_HEADS, 1)
        cf = _tri_dot(lower, gsh)
        cb = _tri_dot(upper, gsh)
        cum = jnp.where(lane_c < 5 * DN_HEADS, cf, cb)
        cur = o_ref[c0:c0 + DN_CHUNK, :]
        o_ref[c0:c0 + DN_CHUNK, :] = jnp.where((lane_c >= 4 * DN_HEADS) & (lane_c < 6 * DN_HEADS), cum, cur)


def _dn_gates(small, a_log, dt_bias):
    m = small.shape[0]
    tm = _row_tile(m, 1088)
    coef = jnp.zeros((LANES,), F32).at[2 * DN_HEADS:4 * DN_HEADS].set(-jnp.exp(a_log.astype(F32)).reshape(-1))
    dtb = jnp.zeros((LANES,), F32).at[2 * DN_HEADS:4 * DN_HEADS].set(dt_bias.astype(F32).reshape(-1))
    return pl.pallas_call(
        functools.partial(_dn_gates_kernel, tm=tm),
        grid=(m // tm,),
        in_specs=[pl.BlockSpec((tm, LANES), lambda i: (i, 0)),
                  pl.BlockSpec((1, LANES), lambda i: (0, 0)),
                  pl.BlockSpec((1, LANES), lambda i: (0, 0))],
        out_specs=pl.BlockSpec((tm, LANES), lambda i: (i, 0)),
        out_shape=jax.ShapeDtypeStruct((m, LANES), F32),
        compiler_params=_cparams("arbitrary"),
        name="dn_gates",
    )(small, coef.reshape(1, LANES), dtb.reshape(1, LANES))


def _neumann_inverse(lm):
    n = lm.shape[0]
    eye = (lax.broadcasted_iota(jnp.int32, (n, n), 0) == lax.broadcasted_iota(jnp.int32, (n, n), 1)).astype(F32)
    inv = eye - lm
    pw = _dot(lm, lm)
    for _ in range(4):
        inv = inv + _dot(inv, pw)
        pw = _dot(pw, pw)
    return inv + _dot(inv, pw)


def _chunk_masks(n):
    ri = lax.broadcasted_iota(jnp.int32, (n, n), 0)
    ci = lax.broadcasted_iota(jnp.int32, (n, n), 1)
    return ((ri > ci, ri >= ci), (ri < ci, ri <= ci))


DN_PAIR = 2 * DN_CHUNK


def _dn_intra_kernel(k_ref, v_ref, col_ref, row_ref, u_ref, w_ref, *, n_pairs):
    ri = lax.broadcasted_iota(jnp.int32, (DN_PAIR, DN_PAIR), 0)
    ci = lax.broadcasted_iota(jnp.int32, (DN_PAIR, DN_PAIR), 1)
    shift = DN_CHUNK.bit_length() - 1
    same = (ri >> shift) == (ci >> shift)
    strict = (same & (ri > ci), same & (ri < ci))

    def body(g, carry):
        r0 = pl.multiple_of(g * DN_PAIR, DN_PAIR)
        k = k_ref[0, pl.ds(r0, DN_PAIR), :]
        v = v_ref[0, pl.ds(r0, DN_PAIR), :]
        for d in range(2):
            beta = col_ref[0, 0, pl.ds(r0, DN_PAIR), d:d + 1]
            gcol = col_ref[0, 0, pl.ds(r0, DN_PAIR), 2 + d:3 + d]
            grow = row_ref[0, 0, pl.ds(g, 1), d:d + 1, :][0]
            decay = jnp.exp(jnp.where(strict[d], gcol - grow, NEG_BIG))
            kb = k * beta
            inv = _neumann_inverse(_dot(kb, k, NT_DIMS) * decay)
            sol = _dot(inv, jnp.concatenate([v * beta, kb * jnp.exp(gcol)], axis=1))
            u_ref[0, d, pl.ds(r0, DN_PAIR), :] = sol[:, 0:DN_HEAD_DIM]
            w_ref[0, d, pl.ds(r0, DN_PAIR), :] = sol[:, DN_HEAD_DIM:2 * DN_HEAD_DIM]
        return carry

    lax.fori_loop(0, n_pairs, body, 0, unroll=2 if n_pairs % 2 == 0 else 1)


def _dn_intra(qkv, cols, grows, nb, rows):
    m = nb * rows
    n_pairs = rows // DN_PAIR
    steps = 2 if n_pairs % 2 == 0 else 1
    pps = n_pairs // steps
    blk = pps * DN_PAIR
    shape = jax.ShapeDtypeStruct((DN_HEADS, 2, m, LANES), F32)
    return pl.pallas_call(
        functools.partial(_dn_intra_kernel, n_pairs=pps),
        grid=(nb, DN_HEADS, steps),
        in_specs=[pl.BlockSpec((1, blk, LANES), lambda b, h, g: (DN_HEADS + h, b * steps + g, 0)),
                  pl.BlockSpec((1, blk, LANES), lambda b, h, g: (2 * DN_HEADS + h, b * steps + g, 0)),
                  pl.BlockSpec((1, 1, blk, 8), lambda b, h, g: (b, h, g, 0)),
                  pl.BlockSpec((1, 1, pps, 2, DN_PAIR), lambda b, h, g: (b, h, g, 0, 0))],
        out_specs=[pl.BlockSpec((1, 2, blk, LANES), lambda b, h, g: (h, 0, b * steps + g, 0)),
                   pl.BlockSpec((1, 2, blk, LANES), lambda b, h, g: (h, 0, b * steps + g, 0))],
        out_shape=[shape, shape],
        compiler_params=_cparams("arbitrary", "arbitrary", "arbitrary"),
        name="dn_intra",
    )(qkv, qkv, cols, grows)


def _dn_scan_kernel(q_ref, k_ref, u_ref, w_ref, col_ref, row_ref, z_ref, nw_ref, o_ref, acc_ref, *, n_chunks):
    masks = _chunk_masks(DN_CHUNK)
    acc_ref[...] = jnp.zeros_like(acc_ref)
    n_x = n_chunks[0]
    n_all = n_chunks[0] + n_chunks[1]

    def body(i, carry):
        states = []
        for d in range(2):
            s = carry[d]
            if d == 0:
                c = jnp.where(i < n_chunks[1], n_x + i, i - n_chunks[1])
            else:
                c = n_all - 1 - i
            r0 = pl.multiple_of(c * DN_CHUNK, DN_CHUNK)
            q = q_ref[0, pl.ds(r0, DN_CHUNK), :]
            k = k_ref[0, pl.ds(r0, DN_CHUNK), :]
            u = u_ref[0, d, pl.ds(r0, DN_CHUNK), :]
            w = w_ref[0, d, pl.ds(r0, DN_CHUNK), :]
            gcol = col_ref[0, 0, pl.ds(r0, DN_CHUNK), 2 + d:3 + d]
            grow = row_ref[0, 0, pl.ds(c, 1), d:d + 1, :][0]
            incl = masks[d][1]
            decay = jnp.exp(jnp.where(incl, gcol - grow, NEG_BIG))
            attn = _dot(q, k, NT_DIMS) * decay
            glast = gcol[DN_CHUNK - 1:DN_CHUNK, :] if d == 0 else gcol[0:1, :]
            qg = q * jnp.exp(gcol)
            kg = k * jnp.exp(glast - gcol)
            v_new = u - _dot(w, s)
            o = _dot(qg, s) + _dot(attn, v_new)
            acc_ref[pl.ds(r0, DN_CHUNK), :] += o
            states.append(s * jnp.exp(glast) + _dot(kg, v_new, TN_DIMS))
        return tuple(states)

    s0 = jnp.zeros((DN_HEAD_DIM, DN_HEAD_DIM), F32)
    lax.fori_loop(0, n_all, body, (s0, s0))
    o = acc_ref[...]
    y = o * lax.rsqrt(jnp.mean(o * o, axis=-1, keepdims=True) + NORM_EPS) * nw_ref[...]
    o_ref[0] = (y * _silu(z_ref[0].astype(F32))).astype(o_ref.dtype)


def _dn_scan(qkv, u, w, cols, grows, p, norm_w, nb, n_x, n_c, z_tile0):
    rows = n_x + n_c
    m = nb * rows
    n_chunks = (n_x // DN_CHUNK, n_c // DN_CHUNK)
    return pl.pallas_call(
        functools.partial(_dn_scan_kernel, n_chunks=n_chunks),
        grid=(nb, DN_HEADS),
        in_specs=[pl.BlockSpec((1, rows, LANES), lambda b, h: (h, b, 0)),
                  pl.BlockSpec((1, rows, LANES), lambda b, h: (DN_HEADS + h, b, 0)),
                  pl.BlockSpec((1, 2, rows, LANES), lambda b, h: (h, 0, b, 0)),
                  pl.BlockSpec((1, 2, rows, LANES), lambda b, h: (h, 0, b, 0)),
                  pl.BlockSpec((1, 1, rows, 8), lambda b, h: (b, h, 0, 0)),
                  pl.BlockSpec((1, 1, rows // DN_CHUNK, 2, DN_CHUNK), lambda b, h: (b, h, 0, 0, 0)),
                  pl.BlockSpec((1, rows, LANES), lambda b, h: (z_tile0 + h, b, 0)),
                  pl.BlockSpec((1, LANES), lambda b, h: (0, 0))],
        out_specs=pl.BlockSpec((1, rows, LANES), lambda b, h: (h, b, 0)),
        out_shape=jax.ShapeDtypeStruct((DN_HEADS, m, LANES), BF16),
        scratch_shapes=[pltpu.VMEM((rows, LANES), F32)],
        compiler_params=_cparams("arbitrary", "arbitrary"),
        name="dn_scan",
    )(qkv, qkv, u, w, cols, grows, p, norm_w.reshape(1, LANES).astype(F32))


def _deltanet_mixer(p, small, conv_w, a_log, dt_bias, norm_w, nb, n_x, n_c):
    rows = n_x + n_c
    qkv = _dn_conv(p, conv_w, nb, n_x, n_c)
    gates = _dn_gates(small, a_log, dt_bias).reshape(nb, rows, LANES)
    h = DN_HEADS
    cols = jnp.stack([gates[..., 0:h], gates[..., h:2 * h], gates[..., 4 * h:5 * h], gates[..., 5 * h:6 * h]],
                     axis=-1)
    cols = jnp.pad(cols.transpose(0, 2, 1, 3), ((0, 0), (0, 0), (0, 0), (0, 4)))
    grows = jnp.stack([gates[..., 4 * h:5 * h], gates[..., 5 * h:6 * h]], axis=-1)

    def row_form(width):
        return grows.reshape(nb, rows // width, width, h, 2).transpose(0, 3, 1, 4, 2)

    u, w = _dn_intra(qkv, cols, row_form(DN_PAIR), nb, rows)
    return _dn_scan(qkv, u, w, cols, row_form(DN_CHUNK), p, norm_w, nb, n_x, n_c, QKV_TILES)


def _gla_kernel(q_ref, k_ref, v_ref, r_ref, sm_ref, w2_ref, gb_ref, nw_ref, o_ref,
                gl_ref, acc_ref, st_ref, *, n_x, n_c):
    rows = n_x + n_c
    cs = GLA_CHUNK
    sm = sm_ref[...].astype(BF16)
    for d in range(2):
        z = jnp.dot(sm, w2_ref[d].astype(BF16), preferred_element_type=F32) + gb_ref[d]
        gl_ref[d] = (jnp.minimum(z, 0.0) - jnp.log1p(jnp.exp(-jnp.abs(z)))) * (1.0 / GLA_TAU)
    acc_ref[...] = jnp.zeros_like(acc_ref)
    st_ref[...] = jnp.zeros_like(st_ref)
    sub = GLA_SUB
    n_sub = cs // sub
    ri = lax.broadcasted_iota(jnp.int32, (cs, cs), 0)
    ci = lax.broadcasted_iota(jnp.int32, (cs, cs), 1)
    incl = ((ri >= ci).astype(F32), (ri <= ci).astype(F32))
    nxc, ncc = n_x // cs, n_c // cs
    scale = GLA_DK ** -0.5

    def sub_block(d, s):
        rs = slice(s * sub, (s + 1) * sub)
        if d == 0:
            return rs, slice(0, (s + 1) * sub), (s * sub - 1 if s > 0 else None)
        return rs, slice(s * sub, cs), ((s + 1) * sub if s < n_sub - 1 else None)

    def causal_mask(d, s):
        _, cols, _ = sub_block(d, s)
        width = cols.stop - cols.start
        row = lax.broadcasted_iota(jnp.int32, (sub, width), 0) + s * sub
        col = lax.broadcasted_iota(jnp.int32, (sub, width), 1) + cols.start
        return row >= col if d == 0 else row <= col

    causal = [[causal_mask(d, s) for s in range(n_sub)] for d in range(2)]

    def body(i, carry):
        for d in range(2):
            c = jnp.where(i < ncc, nxc + i, i - ncc) if d == 0 else nxc + ncc - 1 - i
            r0 = pl.multiple_of(c * cs, cs)
            g = gl_ref[d, pl.ds(r0, cs), :]
            cum = _tri_dot(incl[d], g)
            q = jnp.concatenate([q_ref[t, pl.ds(r0, cs), :] for t in range(2)], axis=1)
            k = jnp.concatenate([k_ref[t, pl.ds(r0, cs), :] for t in range(2)], axis=1)
            v = jnp.concatenate([v_ref[t, pl.ds(r0, cs), :] for t in range(4)], axis=1)
            clast = cum[cs - 1:cs, :] if d == 0 else cum[0:1, :]
            st = st_ref[d]
            o_inter = _dot(q * (jnp.exp(cum) * scale), st, NT_DIMS)
            parts = []
            for s in range(n_sub):
                rs, cols, ref_row = sub_block(d, s)
                ref = cum[ref_row:ref_row + 1, :] if ref_row is not None else 0.0
                qs = q[rs] * (jnp.exp(cum[rs] - ref) * scale)
                ks = k[cols] * jnp.exp(ref - cum[cols])
                a = jnp.where(causal[d][s], _dot(qs, ks, NT_DIMS), 0.0)
                parts.append(_dot(a, v[cols]))
            o = o_inter + jnp.concatenate(parts, axis=0)
            st_ref[d] = st * jnp.exp(clast) + _dot(v, k * jnp.exp(clast - cum), TN_DIMS)

            @pl.when(c < nxc)
            def _():
                acc_ref[pl.ds(r0, cs), :] += o
        return carry

    lax.fori_loop(0, rows // cs, body, 0)
    o = acc_ref[...]
    y = o * lax.rsqrt(jnp.mean(o * o, axis=-1, keepdims=True) + NORM_EPS) * nw_ref[...]
    for t in range(GLA_DV // LANES):
        r = r_ref[t, 0:n_x, :].astype(F32)
        o_ref[t] = (y[:, t * LANES:(t + 1) * LANES] * _silu(r)).astype(o_ref.dtype)


def _gla_mixer(p, small, gate_w2, gate_b, norm_w, nb, n_x, n_c):
    rows = n_x + n_c
    w2p = jnp.zeros((2, LANES, GLA_HEADS * GLA_DK), F32)
    for d in range(2):
        w2p = w2p.at[d, d * GLA_RANK:(d + 1) * GLA_RANK].set(gate_w2[d].astype(F32))
    qt, vt = GLA_DK // LANES, GLA_DV // LANES
    nq = GLA_HEADS
    return pl.pallas_call(
        functools.partial(_gla_kernel, n_x=n_x, n_c=n_c),
        grid=(nb, GLA_HEADS),
        in_specs=[pl.BlockSpec((qt, rows, LANES), lambda b, h: (h, b, 0), pipeline_mode=pl.Buffered(1)),
                  pl.BlockSpec((qt, rows, LANES), lambda b, h: (nq + h, b, 0), pipeline_mode=pl.Buffered(1)),
                  pl.BlockSpec((vt, rows, LANES), lambda b, h: (nq + h, b, 0), pipeline_mode=pl.Buffered(1)),
                  pl.BlockSpec((vt, rows, LANES), lambda b, h: (2 * nq + h, b, 0), pipeline_mode=pl.Buffered(1)),
                  pl.BlockSpec((rows, LANES), lambda b, h: (b, 0)),
                  pl.BlockSpec((2, LANES, GLA_DK), lambda b, h: (0, 0, h)),
                  pl.BlockSpec((2, 1, GLA_DK), lambda b, h: (0, 0, h)),
                  pl.BlockSpec((1, GLA_DV), lambda b, h: (0, 0))],
        out_specs=pl.BlockSpec((vt, n_x, LANES), lambda b, h: (h, b, 0)),
        out_shape=jax.ShapeDtypeStruct((GLA_HEADS * vt, nb * n_x, LANES), BF16),
        scratch_shapes=[pltpu.VMEM((2, rows, GLA_DK), F32),
                        pltpu.VMEM((n_x, GLA_DV), F32),
                        pltpu.VMEM((2, GLA_DV, GLA_DK), F32)],
        compiler_params=_cparams("arbitrary", "arbitrary"),
        name="gla",
    )(p, p, p, p, small, w2p, gate_b.astype(F32).reshape(2, 1, GLA_HEADS * GLA_DK),
      norm_w.astype(F32).reshape(1, GLA_DV))


def _router_kernel(h_ref, w_ref, e_ref, g_ref):
    logits = _dot(h_ref[...], w_ref[...])
    lane = lax.broadcasted_iota(jnp.int32, logits.shape, 1).astype(F32)
    logits = jnp.where(lane < N_EXPERTS, logits, NEG_BIG)
    m1 = jnp.max(logits, axis=-1, keepdims=True)
    i1 = jnp.min(jnp.where(logits == m1, lane, float(LANES)), axis=-1, keepdims=True)
    rest = jnp.where(lane == i1, NEG_BIG, logits)
    m2 = jnp.max(rest, axis=-1, keepdims=True)
    i2 = jnp.min(jnp.where(rest == m2, lane, float(LANES)), axis=-1, keepdims=True)
    e2 = jnp.exp(m2 - m1)
    den = 1.0 + e2
    e_ref[...] = jnp.where(lane == 0.0, i1, jnp.where(lane == 1.0, i2, 0.0)).astype(jnp.int32)
    g_ref[...] = jnp.where(lane == 0.0, 1.0 / den, jnp.where(lane == 1.0, e2 / den, 0.0))


def _router(h, w_router):
    m = h.shape[0]
    tm = 512
    wp = jnp.pad(w_router.astype(F32), ((0, 0), (0, LANES - N_EXPERTS)))
    return pl.pallas_call(
        _router_kernel,
        grid=(m // tm,),
        in_specs=[pl.BlockSpec((tm, D_MODEL), lambda i: (i, 0)),
                  pl.BlockSpec((D_MODEL, LANES), lambda i: (0, 0))],
        out_specs=[pl.BlockSpec((tm, LANES), lambda i: (i, 0)),
                   pl.BlockSpec((tm, LANES), lambda i: (i, 0))],
        out_shape=[jax.ShapeDtypeStruct((m, LANES), jnp.int32),
                   jax.ShapeDtypeStruct((m, LANES), F32)],
        compiler_params=_cparams("arbitrary"),
        name="router",
    )(h, wp)


def _row_copy(src_ref, src_row, dst_ref, dst_row, sem):
    return pltpu.make_async_copy(src_ref.at[pl.ds(src_row, 1)], dst_ref.at[pl.ds(dst_row, 1)], sem)


def _moe_kernel(be_ref, nused_ref, tok_ref, dst_ref, h_ref, w1_ref, w3_ref, w2_ref, y_ref,
                a32_ref, a16_ref, acc_ref, sem):
    i = pl.program_id(0)
    j = pl.program_id(1)
    nj = pl.num_programs(1)
    base = i * MOE_TM

    @pl.when(i < nused_ref[0])
    def _():
        @pl.when(j == 0)
        def _():
            def start(r, c):
                _row_copy(h_ref, tok_ref[base + r], a32_ref, r, sem).start()
                return c

            def wait(r, c):
                _row_copy(h_ref, tok_ref[base + r], a32_ref, r, sem).wait()
                return c

            lax.fori_loop(0, MOE_TM, start, 0, unroll=8)
            lax.fori_loop(0, MOE_TM, wait, 0, unroll=8)
            a16_ref[...] = a32_ref[...].astype(BF16)

        a = a16_ref[...]
        h1 = jnp.dot(a, w1_ref[0].astype(BF16), preferred_element_type=F32)
        h3 = jnp.dot(a, w3_ref[0].astype(BF16), preferred_element_type=F32)
        hh = (_silu(h1) * h3).astype(BF16)
        part = jnp.dot(hh, w2_ref[0].astype(BF16), preferred_element_type=F32)

        @pl.when(j == 0)
        def _():
            acc_ref[...] = part

        @pl.when(j > 0)
        def _():
            acc_ref[...] += part

        @pl.when(j == nj - 1)
        def _():
            def start(r, c):
                dst = dst_ref[base + r]

                @pl.when(dst >= 0)
                def _():
                    _row_copy(acc_ref, r, y_ref, dst, sem).start()
                return c

            def wait(r, c):
                dst = dst_ref[base + r]

                @pl.when(dst >= 0)
                def _():
                    _row_copy(acc_ref, r, y_ref, dst, sem).wait()
                return c

            lax.fori_loop(0, MOE_TM, start, 0, unroll=8)
            lax.fori_loop(0, MOE_TM, wait, 0, unroll=8)


def _moe(h, experts, w1, w3, w2):
    n_tok = h.shape[0]
    n_asg = n_tok * TOP_K
    n_blocks = -(-n_asg // MOE_TM) + N_EXPERTS
    n_rows = n_blocks * MOE_TM
    th = 256
    nj = EXPERT_HIDDEN // th
    flat_e = experts.reshape(n_asg)
    onehot = (flat_e[:, None] == jnp.arange(N_EXPERTS, dtype=jnp.int32)[None, :]).astype(jnp.int32)
    csum = jnp.cumsum(onehot, axis=0)
    rank = jnp.sum(csum * onehot, axis=1) - 1
    counts = csum[-1]
    padded = (counts + MOE_TM - 1) // MOE_TM * MOE_TM
    padded_end = jnp.cumsum(padded)
    padded_start = padded_end - padded
    dest = padded_start[flat_e] + rank
    asg = jnp.arange(n_asg, dtype=jnp.int32)
    row_tok = jnp.zeros((n_rows,), jnp.int32).at[dest].set(asg // TOP_K)
    row_dst = jnp.full((n_rows,), -1, jnp.int32).at[dest].set(asg)
    blk_start = jnp.arange(n_blocks, dtype=jnp.int32) * MOE_TM
    n_used = (padded_end[-1] // MOE_TM).astype(jnp.int32)
    blk_e = jnp.sum((padded_end[None, :] <= blk_start[:, None]).astype(jnp.int32), axis=1)
    last_e = jnp.sum((padded_end <= (n_used - 1) * MOE_TM).astype(jnp.int32))
    blk_e = jnp.where(blk_start < padded_end[-1], blk_e, last_e).astype(jnp.int32)

    def w_col(i, j, be, nu, *_):
        return jnp.where(i < nu[0], j, nj - 1)

    grid_spec = pltpu.PrefetchScalarGridSpec(
        num_scalar_prefetch=4,
        grid=(n_blocks, nj),
        in_specs=[pl.BlockSpec(memory_space=pl.ANY),
                  pl.BlockSpec((1, D_MODEL, th), lambda i, j, be, *s: (be[i], 0, w_col(i, j, be, *s))),
                  pl.BlockSpec((1, D_MODEL, th), lambda i, j, be, *s: (be[i], 0, w_col(i, j, be, *s))),
                  pl.BlockSpec((1, th, D_MODEL), lambda i, j, be, *s: (be[i], w_col(i, j, be, *s), 0))],
        out_specs=pl.BlockSpec(memory_space=pl.ANY),
        scratch_shapes=[pltpu.VMEM((MOE_TM, D_MODEL), F32),
                        pltpu.VMEM((MOE_TM, D_MODEL), BF16),
                        pltpu.VMEM((MOE_TM, D_MODEL), F32),
                        pltpu.SemaphoreType.DMA(())],
    )
    return pl.pallas_call(
        _moe_kernel,
        grid_spec=grid_spec,
        out_shape=jax.ShapeDtypeStruct((n_asg, D_MODEL), F32),
        compiler_params=_cparams("arbitrary", "arbitrary"),
        name="moe",
    )(blk_e, n_used.reshape(1), row_tok, row_dst, h, w1, w3, w2)


def kernel(x, c, ctx, c_ctx, ada_w, ada_b, ln1_g, ln1_b, ln2_g, ln2_b, e_w_in, s5_lam_re, s5_lam_im, s5_log_step, s5_b_re, s5_b_im, s5_c_re, s5_c_im, s5_d, dn_conv, dn_a_log, dn_dt_bias, dn_norm_w, e_w_out, ffn_w1, ffn_w3, ffn_w2, o_w_in, gla_w2, gla_b, gla_norm_w, o_w_out, moe_router, moe_w1, moe_w3, moe_w2):
    nb, n_x, _ = x.shape
    n_c = ctx.shape[1]
    rows = n_x + n_c
    m_all = nb * rows
    assert nb < 8 and n_x % (GRID_W * 8) == 0 and n_c % ROW_BLK == 0 and n_x % ROW_BLK == 0

    c8 = jnp.zeros((8, D_MODEL), F32).at[:nb].set(c).at[nb].set(c_ctx)
    mods = _adaln(c8, ada_w, ada_b).reshape(DEPTH, 8, N_MOD, D_MODEL)
    xc = jnp.concatenate([x, ctx], axis=1)

    h = _modulate(xc, mods, 0, n_x)
    p, small = _mm_in(h.reshape(m_all, D_MODEL), e_w_in[0])
    tables = _s5_tables(s5_lam_re[0].astype(F32), s5_lam_im[0].astype(F32), s5_log_step[0].astype(F32),
                        s5_b_re[0].astype(F32), s5_b_im[0].astype(F32), s5_c_re[0].astype(F32),
                        s5_c_im[0].astype(F32), s5_d[0])
    u_tile0 = (3 * DN_WIDTH + DN_WIDTH) // LANES
    y_s5 = _s5_mixer(p, tables, nb, n_x // S5_L, n_c // S5_L, u_tile0, u_tile0 + S5_TILES)
    y_dn = _deltanet_mixer(p, small, dn_conv[0], dn_a_log[0], dn_dt_bias[0], dn_norm_w[0], nb, n_x, n_c)
    mix = _mm_out([y_s5, y_dn], e_w_out[0]).reshape(nb, rows, D_MODEL)
    xc, h = _resid_ln(xc, mix, mods, 0, 2, 0, 3, ln1_g[0], ln1_b[0], n_x, rows, h_dtype=BF16)
    f = _ffn(h.reshape(m_all, D_MODEL), ffn_w1[0], ffn_w3[0], ffn_w2[0]).reshape(nb, rows, D_MODEL)
    xc, h = _resid_ln(xc, f, mods, 0, 5, 1, 0, ln2_g[0], ln2_b[0], n_x, rows, h_dtype=BF16)

    p, small = _mm_in(h.reshape(m_all, D_MODEL), o_w_in[0])
    y_gla = _gla_mixer(p, small, gla_w2[0], gla_b[0], gla_norm_w[0], nb, n_x, n_c)
    mix = _mm_out([y_gla], o_w_out[0]).reshape(nb, n_x, D_MODEL)
    xl, h = _resid_ln(xc, mix, mods, 1, 2, 1, 3, ln1_g[1], ln1_b[1], n_x, n_x, h_dtype=F32)
    hf = h.reshape(nb * n_x, D_MODEL)
    experts, gates = _router(hf, moe_router[0])
    y = _moe(hf, experts[:, :TOP_K], moe_w1[0], moe_w3[0], moe_w2[0])
    out, _ = _resid_ln(xl, y.reshape(nb, n_x, TOP_K * D_MODEL), mods, 1, 5, 1, 0, ln2_g[1], ln2_b[1], n_x, n_x,
                       moe_gates=gates[:, :TOP_K].reshape(nb, n_x, TOP_K))
    return out
```

```python
import functools
import math

import jax
import jax.numpy as jnp
from jax import lax
from jax.experimental import pallas as pl
from jax.experimental.pallas import tpu as pltpu

F32 = jnp.float32
BF16 = jnp.bfloat16

LANES = 128
VMEM_LIMIT = 60 * 1024 * 1024

D_MODEL = 2048
DEPTH = 2
ALPHA = (2 * DEPTH) ** 0.25
NORM_EPS = 1e-6
N_MOD = 6
GRID_W = 64

S5_WIDTH = D_MODEL // 2
S5_GROUP = 16
S5_GROUPS = S5_WIDTH // S5_GROUP
S5_STATE = 64
S5_L = 16
S5_TILES = S5_WIDTH // LANES
S5_GPT = LANES // S5_GROUP
S5_SW = S5_GPT * S5_STATE

DN_WIDTH = D_MODEL // 2
DN_HEAD_DIM = 128
DN_HEADS = DN_WIDTH // DN_HEAD_DIM
DN_CHUNK = 64
QKV_TILES = 3 * DN_WIDTH // LANES

GLA_HEADS = 4
GLA_DK = 256
GLA_DV = 512
GLA_RANK = 16
GLA_TAU = 16.0
GLA_CHUNK = 64
GLA_SUB = 16

MAIN_COLS = 6144
FFN_HIDDEN = 5632
N_EXPERTS = 8
TOP_K = 2
EXPERT_HIDDEN = 7168
MOE_TM = 1024
ROW_BLK = 256
NEG_BIG = -1e30

NT_DIMS = (((1,), (1,)), ((), ()))
TN_DIMS = (((0,), (0,)), ((), ()))


def _cparams(*sem):
    return pltpu.CompilerParams(dimension_semantics=sem, vmem_limit_bytes=VMEM_LIMIT)


def _dot(a, b, dims=None):
    a = a.astype(BF16)
    b = b.astype(BF16)
    if dims is None:
        return jnp.dot(a, b, preferred_element_type=F32)
    return lax.dot_general(a, b, dims, preferred_element_type=F32)


def _tri_dot(tri, g):
    g1 = g.astype(BF16)
    r1 = g - g1.astype(F32)
    g2 = r1.astype(BF16)
    g3 = (r1 - g2.astype(F32)).astype(BF16)
    t = tri.astype(BF16)
    return (jnp.dot(t, g1, preferred_element_type=F32) + jnp.dot(t, g2, preferred_element_type=F32)
            + jnp.dot(t, g3, preferred_element_type=F32))


def _silu(x):
    return x * jax.nn.sigmoid(x)


def _softplus(x):
    return jnp.maximum(x, 0.0) + jnp.log1p(jnp.exp(-jnp.abs(x)))


def _adaln_kernel(c_ref, w_ref, b_ref, o_ref):
    o_ref[0] = _dot(_silu(c_ref[...]), w_ref[0]) + b_ref[0]


def _adaln(c8, ada_w, ada_b):
    tn = 1024
    n = N_MOD * D_MODEL
    return pl.pallas_call(
        _adaln_kernel,
        grid=(DEPTH, n // tn),
        in_specs=[pl.BlockSpec((8, D_MODEL), lambda l, j: (0, 0)),
                  pl.BlockSpec((1, D_MODEL, tn), lambda l, j: (l, 0, j)),
                  pl.BlockSpec((1, 1, tn), lambda l, j: (l, 0, j))],
        out_specs=pl.BlockSpec((1, 8, tn), lambda l, j: (l, 0, j)),
        out_shape=jax.ShapeDtypeStruct((DEPTH, 8, n), F32),
        compiler_params=_cparams("arbitrary", "arbitrary"),
        name="adaln",
    )(c8, ada_w, ada_b.reshape(DEPTH, 1, n))


def _mod_row(nb, nx_blocks):
    return lambda b, r: jnp.where(r < nx_blocks, b, nb)


def _mod_kernel(x_ref, m_ref, o_ref):
    sh = m_ref[0, 0, 0:1, :]
    sc = m_ref[0, 0, 1:2, :]
    o_ref[0] = (x_ref[0] * (1.0 + sc) + sh).astype(o_ref.dtype)


def _modulate(xc, mods, layer, nx):
    nb, rows, _ = xc.shape
    row = _mod_row(nb, nx // ROW_BLK)
    return pl.pallas_call(
        _mod_kernel,
        grid=(nb, rows // ROW_BLK),
        in_specs=[pl.BlockSpec((1, ROW_BLK, D_MODEL), lambda b, r: (b, r, 0)),
                  pl.BlockSpec((1, 1, N_MOD, D_MODEL), lambda b, r: (layer, row(b, r), 0, 0))],
        out_specs=pl.BlockSpec((1, ROW_BLK, D_MODEL), lambda b, r: (b, r, 0)),
        out_shape=jax.ShapeDtypeStruct(xc.shape, BF16),
        compiler_params=_cparams("arbitrary", "arbitrary"),
        name="modulate",
    )(xc, mods)


def _ln_kernel(*refs, gate_row, moe, emit_h):
    it = iter(refs)
    x_ref, y_ref = next(it), next(it)
    gw_ref = next(it) if moe else None
    mg_ref, mn_ref, g_ref, b_ref, xo_ref = next(it), next(it), next(it), next(it), next(it)
    x = x_ref[0]
    if moe:
        gw = gw_ref[0]
        y = gw[:, 0:1] * y_ref[0, :, 0:D_MODEL] + gw[:, 1:2] * y_ref[0, :, D_MODEL:2 * D_MODEL]
    else:
        y = y_ref[0].astype(F32)
    v = ALPHA * x + mg_ref[0, 0, gate_row:gate_row + 1, :] * y
    mu = jnp.mean(v, axis=-1, keepdims=True)
    vc = v - mu
    var = jnp.mean(vc * vc, axis=-1, keepdims=True)
    xn = vc * lax.rsqrt(var + NORM_EPS) * g_ref[...] + b_ref[...]
    xo_ref[0] = xn
    if emit_h:
        ho_ref = next(it)
        sh = mn_ref[0, 0, 0:1, :]
        sc = mn_ref[0, 0, 1:2, :]
        ho_ref[0] = (xn * (1.0 + sc) + sh).astype(ho_ref.dtype)


def _resid_ln(xc, y, mods, gate_layer, gate_row, next_layer, next_row0, ln_g, ln_b, nx, rows_out,
              h_dtype=None, moe_gates=None):
    nb = xc.shape[0]
    row = _mod_row(nb, nx // ROW_BLK)
    moe = moe_gates is not None
    emit_h = h_dtype is not None
    mods_next = mods[:, :, next_row0:next_row0 + 2, :] if emit_h else mods[:, :, 0:2, :]
    in_specs = [pl.BlockSpec((1, ROW_BLK, D_MODEL), lambda b, r: (b, r, 0)),
                pl.BlockSpec((1, ROW_BLK, y.shape[-1]), lambda b, r: (b, r, 0))]
    args = [xc, y]
    if moe:
        in_specs.append(pl.BlockSpec((1, ROW_BLK, TOP_K), lambda b, r: (b, r, 0)))
        args.append(moe_gates)
    in_specs += [pl.BlockSpec((1, 1, N_MOD, D_MODEL), lambda b, r: (gate_layer, row(b, r), 0, 0)),
                 pl.BlockSpec((1, 1, 2, D_MODEL), lambda b, r: (next_layer, row(b, r), 0, 0)),
                 pl.BlockSpec((1, D_MODEL), lambda b, r: (0, 0)),
                 pl.BlockSpec((1, D_MODEL), lambda b, r: (0, 0))]
    args += [mods, mods_next, ln_g.reshape(1, D_MODEL), ln_b.reshape(1, D_MODEL)]
    out_shape = [jax.ShapeDtypeStruct((nb, rows_out, D_MODEL), F32)]
    out_specs = [pl.BlockSpec((1, ROW_BLK, D_MODEL), lambda b, r: (b, r, 0))]
    if emit_h:
        out_shape.append(jax.ShapeDtypeStruct((nb, rows_out, D_MODEL), h_dtype))
        out_specs.append(pl.BlockSpec((1, ROW_BLK, D_MODEL), lambda b, r: (b, r, 0)))
    res = pl.pallas_call(
        functools.partial(_ln_kernel, gate_row=gate_row, moe=moe, emit_h=emit_h),
        grid=(nb, rows_out // ROW_BLK),
        in_specs=in_specs, out_specs=out_specs, out_shape=out_shape,
        compiler_params=_cparams("arbitrary", "arbitrary"),
        name="resid_ln",
    )(*args)
    return res if emit_h else (res[0], None)


def _mm_in_kernel(a_ref, w_ref, ws_ref, o_ref, os_ref, *, tn):
    a = a_ref[...]
    res = jnp.dot(a, w_ref[...].astype(BF16), preferred_element_type=F32)
    for t in range(tn // LANES):
        o_ref[t] = res[:, t * LANES:(t + 1) * LANES].astype(o_ref.dtype)

    @pl.when(pl.program_id(1) == 0)
    def _():
        os_ref[...] = jnp.dot(a, ws_ref[...].astype(BF16), preferred_element_type=F32)


def _row_tile(m, cap):
    best = 16
    for t in range(16, cap + 1, 16):
        if m % t == 0:
            best = t
    return best


def _mm_in(a, w):
    m = a.shape[0]
    tm = _row_tile(m, 1088)
    tn = 512
    w_small = jnp.pad(w[:, MAIN_COLS:], ((0, 0), (0, LANES - (w.shape[1] - MAIN_COLS))))
    return pl.pallas_call(
        functools.partial(_mm_in_kernel, tn=tn),
        grid=(m // tm, MAIN_COLS // tn),
        in_specs=[pl.BlockSpec((tm, D_MODEL), lambda i, j: (i, 0)),
                  pl.BlockSpec((D_MODEL, tn), lambda i, j: (0, j)),
                  pl.BlockSpec((D_MODEL, LANES), lambda i, j: (0, 0))],
        out_specs=[pl.BlockSpec((tn // LANES, tm, LANES), lambda i, j: (j, i, 0)),
                   pl.BlockSpec((tm, LANES), lambda i, j: (i, 0))],
        out_shape=[jax.ShapeDtypeStruct((MAIN_COLS // LANES, m, LANES), BF16),
                   jax.ShapeDtypeStruct((m, LANES), F32)],
        compiler_params=_cparams("arbitrary", "arbitrary"),
        name="mm_in",
    )(a, w, w_small)


def _mm_out_kernel(*refs, n_a):
    a_refs, w_ref, o_ref = refs[:n_a], refs[n_a], refs[n_a + 1]
    a = jnp.concatenate([r[t] for r in a_refs for t in range(r.shape[0])], axis=1)
    o_ref[...] = jnp.dot(a, w_ref[...].astype(BF16), preferred_element_type=F32)


def _mm_out(a_list, w):
    m = a_list[0].shape[1]
    tm = _row_tile(m, 1088)
    tn = 512
    in_specs = [pl.BlockSpec((a.shape[0], tm, LANES), lambda i, j: (0, i, 0)) for a in a_list]
    in_specs.append(pl.BlockSpec((D_MODEL, tn), lambda i, j: (0, j)))
    return pl.pallas_call(
        functools.partial(_mm_out_kernel, n_a=len(a_list)),
        grid=(m // tm, D_MODEL // tn),
        in_specs=in_specs,
        out_specs=pl.BlockSpec((tm, tn), lambda i, j: (i, j)),
        out_shape=jax.ShapeDtypeStruct((m, D_MODEL), F32),
        compiler_params=_cparams("arbitrary", "arbitrary"),
        name="mm_out",
    )(*a_list, w)


def _ffn_kernel(a_ref, w1_ref, w3_ref, w2_ref, o_ref):
    j = pl.program_id(1)
    a = a_ref[...]
    h1 = jnp.dot(a, w1_ref[...].astype(BF16), preferred_element_type=F32)
    h3 = jnp.dot(a, w3_ref[...].astype(BF16), preferred_element_type=F32)
    hh = (_silu(h1) * h3).astype(BF16)
    part = jnp.dot(hh, w2_ref[...].astype(BF16), preferred_element_type=F32)

    @pl.when(j == 0)
    def _():
        o_ref[...] = part

    @pl.when(j > 0)
    def _():
        o_ref[...] += part


def _ffn(a, w1, w3, w2):
    m = a.shape[0]
    hidden = w1.shape[1]
    tm = _row_tile(m, 1088)
    th = 256
    return pl.pallas_call(
        _ffn_kernel,
        grid=(m // tm, hidden // th),
        in_specs=[pl.BlockSpec((tm, D_MODEL), lambda i, j: (i, 0)),
                  pl.BlockSpec((D_MODEL, th), lambda i, j: (0, j)),
                  pl.BlockSpec((D_MODEL, th), lambda i, j: (0, j)),
                  pl.BlockSpec((th, D_MODEL), lambda i, j: (j, 0))],
        out_specs=pl.BlockSpec((tm, D_MODEL), lambda i, j: (i, 0)),
        out_shape=jax.ShapeDtypeStruct((m, D_MODEL), F32),
        compiler_params=_cparams("arbitrary", "arbitrary"),
        name="ffn",
    )(a, w1, w3, w2)


def _s5_tables(lam_re, lam_im, log_step, b_re, b_im, c_re, c_im, d_skip):
    hp = lax.Precision.HIGHEST
    tau = jnp.arange(S5_L + 1, dtype=F32)[:, None, None]
    lag_k, ab, ca, a_l = [], [], [], []
    for d in range(2):
        delta = jnp.exp(log_step[d])[:, None]
        lr, li = lam_re[d], lam_im[d]
        mag = jnp.exp(lr * delta * tau)
        ang = li * delta * tau
        are, aim = mag * jnp.cos(ang), mag * jnp.sin(ang)
        den = jnp.square(lr) + jnp.square(li)
        f_re = ((are[1] - 1.0) * lr + aim[1] * li) / den
        f_im = (aim[1] * lr - (are[1] - 1.0) * li) / den
        bbr = f_re[..., None] * b_re[d] - f_im[..., None] * b_im[d]
        bbi = f_re[..., None] * b_im[d] + f_im[..., None] * b_re[d]
        car = c_re[d][None] * are[:, :, None, :] - c_im[d][None] * aim[:, :, None, :]
        cai = c_re[d][None] * aim[:, :, None, :] + c_im[d][None] * are[:, :, None, :]
        lag_k.append(jnp.einsum('tghp,gpk->tghk', car, bbr, precision=hp)
                     - jnp.einsum('tghp,gpk->tghk', cai, bbi, precision=hp))
        ab.append((are[..., None] * bbr[None] - aim[..., None] * bbi[None],
                   are[..., None] * bbi[None] + aim[..., None] * bbr[None]))
        ca.append((car, cai))
        a_l.append((are[S5_L], aim[S5_L]))
    eye = jnp.eye(S5_GPT, dtype=F32)
    kf, kb = lag_k
    lag = jnp.concatenate([kb[S5_L - 1:0:-1], (kf[0] + kb[0])[None], kf[1:S5_L]], axis=0)
    lag = lag.reshape(2 * S5_L - 1, S5_TILES, S5_GPT, S5_GROUP, S5_GROUP)
    wlag = jnp.einsum('ljghk,gf->jgklfh', lag, eye).reshape(S5_TILES, LANES, (2 * S5_L - 1) * LANES)
    wb = jnp.stack([ab[0][0][:S5_L][::-1], ab[0][1][:S5_L][::-1], ab[1][0][:S5_L], ab[1][1][:S5_L]], axis=0)
    wb = wb.reshape(4, S5_L, S5_TILES, S5_GPT, S5_STATE, S5_GROUP)
    wb = jnp.einsum('csjgpk,gf->jsgkcfp', wb, eye).reshape(S5_TILES, S5_L * LANES, 4 * S5_SW)
    wc = jnp.stack([ca[0][0][1:], -ca[0][1][1:], ca[1][0][S5_L:0:-1], -ca[1][1][S5_L:0:-1]], axis=0)
    wc = wc.reshape(4, S5_L, S5_TILES, S5_GPT, S5_GROUP, S5_STATE)
    wc = jnp.einsum('ctjghp,gf->jcfptgh', wc, eye).reshape(S5_TILES, 4 * S5_SW, S5_L * LANES)
    a16 = jnp.stack([a_l[0][0], a_l[0][1], a_l[1][0], a_l[1][1]], axis=0)
    a16 = a16.reshape(4, S5_TILES, S5_SW).transpose(1, 0, 2)
    dsk = jnp.tile(d_skip.astype(F32).reshape(S5_TILES, 1, LANES), (1, 1, S5_L))
    return wlag.astype(BF16), wb.astype(BF16), wc.astype(BF16), a16, dsk


def _s5_kernel(u_ref, g_ref, wlag_ref, wb_ref, wc_ref, a16_ref, d_ref, o_ref,
               wt_ref, s_ref, hin_ref, *, n_x, n_c):
    rows = n_x + n_c

    @pl.when(pl.program_id(1) == 0)
    def _():
        for s in range(S5_L):
            wt_ref[s * LANES:(s + 1) * LANES, :] = wlag_ref[0, :, (S5_L - 1 - s) * LANES:(2 * S5_L - 1 - s) * LANES]

    ub = u_ref[0]
    u = ub.astype(F32)
    s_ref[...] = jnp.dot(ub, wb_ref[0], preferred_element_type=F32)
    a = a16_ref[0]
    afr, afi, abr, abi = a[0:1], a[1:2], a[2:3], a[3:4]
    w = S5_SW

    sub = 8
    tiles_x, tiles_c = n_x // sub, n_c // sub

    def tile_scan(t, hr, hi, ar, ai, lane0, forward):
        r0 = pl.multiple_of(t * sub, sub)
        sr = s_ref[pl.ds(r0, sub), lane0:lane0 + w]
        si = s_ref[pl.ds(r0, sub), lane0 + w:lane0 + 2 * w]
        in_r, in_i = [None] * sub, [None] * sub
        for r in (range(sub) if forward else range(sub - 1, -1, -1)):
            in_r[r], in_i[r] = hr, hi
            hr, hi = ar * hr - ai * hi + sr[r:r + 1], ar * hi + ai * hr + si[r:r + 1]
        hin_ref[pl.ds(r0, sub), lane0:lane0 + w] = jnp.concatenate(in_r, axis=0)
        hin_ref[pl.ds(r0, sub), lane0 + w:lane0 + 2 * w] = jnp.concatenate(in_i, axis=0)
        return hr, hi

    def body(i, carry):
        hfr, hfi, hbr, hbi = carry
        tf = jnp.where(i < tiles_c, tiles_x + i, i - tiles_c)
        tb = tiles_x + tiles_c - 1 - i
        hfr, hfi = tile_scan(tf, hfr, hfi, afr, afi, 0, True)
        hbr, hbi = tile_scan(tb, hbr, hbi, abr, abi, 2 * w, False)
        return hfr, hfi, hbr, hbi

    z = jnp.zeros((1, w), F32)
    lax.fori_loop(0, tiles_x + tiles_c, body, (z, z, z, z))
    y = (jnp.dot(ub, wt_ref[...], preferred_element_type=F32)
         + jnp.dot(hin_ref[...].astype(BF16), wc_ref[0], preferred_element_type=F32))
    y = y + d_ref[0] * u
    gelu = 0.5 * y * (1.0 + jnp.tanh(math.sqrt(2.0 / math.pi) * (y + 0.044715 * (y * y * y))))
    o_ref[0] = (gelu * jax.nn.sigmoid(g_ref[0].astype(F32))).astype(o_ref.dtype)


def _s5_mixer(p, tables, nb, n_x, n_c, u_tile0, g_tile0):
    wlag, wb, wc, a16, dsk = tables
    rows = n_x + n_c
    width = S5_L * LANES
    pv = p.reshape(p.shape[0], nb * rows, width)
    return pl.pallas_call(
        functools.partial(_s5_kernel, n_x=n_x, n_c=n_c),
        grid=(S5_TILES, nb),
        in_specs=[pl.BlockSpec((1, rows, width), lambda j, b: (u_tile0 + j, b, 0)),
                  pl.BlockSpec((1, rows, width), lambda j, b: (g_tile0 + j, b, 0)),
                  pl.BlockSpec((1, LANES, wlag.shape[2]), lambda j, b: (j, 0, 0), pipeline_mode=pl.Buffered(1)),
                  pl.BlockSpec((1, width, 4 * S5_SW), lambda j, b: (j, 0, 0), pipeline_mode=pl.Buffered(1)),
                  pl.BlockSpec((1, 4 * S5_SW, width), lambda j, b: (j, 0, 0), pipeline_mode=pl.Buffered(1)),
                  pl.BlockSpec((1, 4, S5_SW), lambda j, b: (j, 0, 0)),
                  pl.BlockSpec((1, 1, width), lambda j, b: (j, 0, 0))],
        out_specs=pl.BlockSpec((1, rows, width), lambda j, b: (j, b, 0)),
        out_shape=jax.ShapeDtypeStruct((S5_TILES, nb * rows, width), BF16),
        scratch_shapes=[pltpu.VMEM((width, width), BF16),
                        pltpu.VMEM((rows, 4 * S5_SW), F32),
                        pltpu.VMEM((rows, 4 * S5_SW), F32)],
        compiler_params=_cparams("arbitrary", "arbitrary"),
        name="s5",
    )(pv, pv, wlag, wb, wc, a16, dsk).reshape(S5_TILES, nb * rows * S5_L, LANES)


def _dn_conv_kernel(x_ref, w_ref, o_ref, xp_ref, cp_ref, *, n_x, n_c):
    pad = 72
    blk = 512
    tile = pl.program_id(0)
    w = w_ref[...]
    xp_ref[0:pad, :] = jnp.zeros((pad, LANES), F32)
    xp_ref[pad + n_x:pad + n_x + pad, :] = jnp.zeros((pad, LANES), F32)
    xp_ref[pad:pad + n_x, :] = x_ref[0, 0:n_x, :].astype(F32)
    cp_ref[0:8, :] = jnp.zeros((8, LANES), F32)
    cp_ref[8 + n_c:16 + n_c, :] = jnp.zeros((8, LANES), F32)
    cp_ref[8:8 + n_c, :] = x_ref[0, n_x:n_x + n_c, :].astype(F32)
    is_qk = tile < 2 * DN_HEADS
    scale = jnp.where(tile < DN_HEADS, DN_HEAD_DIM ** -0.5, 1.0)

    def finish(y):
        y = _silu(y)
        nrm = y * lax.rsqrt(jnp.sum(y * y, axis=-1, keepdims=True) + NORM_EPS) * scale
        return jnp.where(is_qk, nrm, y)

    col = lax.broadcasted_iota(jnp.int32, (blk, LANES), 0) & (GRID_W - 1)
    not_first = col != 0
    not_last = col != GRID_W - 1
    for r0 in range(0, n_x, blk):
        left = mid = right = None
        for kh in range(3):
            base = pad + r0 + (kh - 1) * GRID_W
            tl = xp_ref[base - 1:base - 1 + blk, :] * w[3 * kh:3 * kh + 1, :]
            tm = xp_ref[base:base + blk, :] * w[3 * kh + 1:3 * kh + 2, :]
            tr = xp_ref[base + 1:base + 1 + blk, :] * w[3 * kh + 2:3 * kh + 3, :]
            left = tl if left is None else left + tl
            mid = tm if mid is None else mid + tm
            right = tr if right is None else right + tr
        y = jnp.where(not_first, left, 0.0) + mid + jnp.where(not_last, right, 0.0)
        o_ref[0, r0:r0 + blk, :] = finish(y)
    yc = (cp_ref[7:7 + n_c, :] * w[3:4, :] + cp_ref[8:8 + n_c, :] * w[4:5, :]
          + cp_ref[9:9 + n_c, :] * w[5:6, :])
    o_ref[0, n_x:n_x + n_c, :] = finish(yc)


def _dn_conv(p, conv_w, nb, n_x, n_c):
    rows = n_x + n_c
    w9 = conv_w.reshape(9, 3 * DN_WIDTH)
    return pl.pallas_call(
        functools.partial(_dn_conv_kernel, n_x=n_x, n_c=n_c),
        grid=(QKV_TILES, nb),
        in_specs=[pl.BlockSpec((1, rows, LANES), lambda t, b: (t, b, 0)),
                  pl.BlockSpec((9, LANES), lambda t, b: (0, t))],
        out_specs=pl.BlockSpec((1, rows, LANES), lambda t, b: (t, b, 0)),
        out_shape=jax.ShapeDtypeStruct((QKV_TILES, nb * rows, LANES), F32),
        scratch_shapes=[pltpu.VMEM((n_x + 144, LANES), F32), pltpu.VMEM((n_c + 16, LANES), F32)],
        compiler_params=_cparams("arbitrary", "arbitrary"),
        name="dn_conv",
    )(p, w9)


def _dn_gates_kernel(s_ref, coef_ref, dtb_ref, o_ref, *, tm):
    s = s_ref[...]
    lane = lax.broadcasted_iota(jnp.int32, (tm, LANES), 1)
    beta = jax.nn.sigmoid(s)
    g = coef_ref[...] * _softplus(s + dtb_ref[...])
    o_ref[...] = jnp.where(lane < 2 * DN_HEADS, beta, g)
    ri = lax.broadcasted_iota(jnp.int32, (DN_CHUNK, DN_CHUNK), 0)
    ci = lax.broadcasted_iota(jnp.int32, (DN_CHUNK, DN_CHUNK), 1)
    lower = (ri >= ci).astype(F32)
    upper = (ri <= ci).astype(F32)
    lane_c = lax.broadcasted_iota(jnp.int32, (DN_CHUNK, LANES), 1)
    for c0 in range(0, tm, DN_CHUNK):
        gc = g[c0:c0 + DN_CHUNK, :]
        gsh = pltpu.roll(gc, 2 * DN_HEADS, 1)
        cf = _tri_dot(lower, gsh)
        cb = _tri_dot(upper, gsh)
        cum = jnp.where(lane_c < 5 * DN_HEADS, cf, cb)
        cur = o_ref[c0:c0 + DN_CHUNK, :]
        o_ref[c0:c0 + DN_CHUNK, :] = jnp.where((lane_c >= 4 * DN_HEADS) & (lane_c < 6 * DN_HEADS), cum, cur)


def _dn_gates(small, a_log, dt_bias):
    m = small.shape[0]
    tm = _row_tile(m, 1088)
    coef = jnp.zeros((LANES,), F32).at[2 * DN_HEADS:4 * DN_HEADS].set(-jnp.exp(a_log.astype(F32)).reshape(-1))
    dtb = jnp.zeros((LANES,), F32).at[2 * DN_HEADS:4 * DN_HEADS].set(dt_bias.astype(F32).reshape(-1))
    return pl.pallas_call(
        functools.partial(_dn_gates_kernel, tm=tm),
        grid=(m // tm,),
        in_specs=[pl.BlockSpec((tm, LANES), lambda i: (i, 0)),
                  pl.BlockSpec((1, LANES), lambda i: (0, 0)),
                  pl.BlockSpec((1, LANES), lambda i: (0, 0))],
        out_specs=pl.BlockSpec((tm, LANES), lambda i: (i, 0)),
        out_shape=jax.ShapeDtypeStruct((m, LANES), F32),
        compiler_params=_cparams("arbitrary"),
        name="dn_gates",
    )(small, coef.reshape(1, LANES), dtb.reshape(1, LANES))


def _neumann_inverse(lm):
    n = lm.shape[0]
    eye = (lax.broadcasted_iota(jnp.int32, (n, n), 0) == lax.broadcasted_iota(jnp.int32, (n, n), 1)).astype(F32)
    inv = eye - lm
    pw = _dot(lm, lm)
    for _ in range(4):
        inv = inv + _dot(inv, pw)
        pw = _dot(pw, pw)
    return inv + _dot(inv, pw)


def _chunk_masks(n):
    ri = lax.broadcasted_iota(jnp.int32, (n, n), 0)
    ci = lax.broadcasted_iota(jnp.int32, (n, n), 1)
    return ((ri > ci, ri >= ci), (ri < ci, ri <= ci))


DN_PAIR = 2 * DN_CHUNK


def _dn_intra_kernel(k_ref, v_ref, col_ref, row_ref, u_ref, w_ref, *, n_pairs):
    ri = lax.broadcasted_iota(jnp.int32, (DN_PAIR, DN_PAIR), 0)
    ci = lax.broadcasted_iota(jnp.int32, (DN_PAIR, DN_PAIR), 1)
    shift = DN_CHUNK.bit_length() - 1
    same = (ri >> shift) == (ci >> shift)
    strict = (same & (ri > ci), same & (ri < ci))

    def body(g, carry):
        r0 = pl.multiple_of(g * DN_PAIR, DN_PAIR)
        k = k_ref[0, pl.ds(r0, DN_PAIR), :]
        v = v_ref[0, pl.ds(r0, DN_PAIR), :]
        for d in range(2):
            beta = col_ref[0, 0, pl.ds(r0, DN_PAIR), d:d + 1]
            gcol = col_ref[0, 0, pl.ds(r0, DN_PAIR), 2 + d:3 + d]
            grow = row_ref[0, 0, pl.ds(g, 1), d:d + 1, :][0]
            decay = jnp.exp(jnp.where(strict[d], gcol - grow, NEG_BIG))
            kb = k * beta
            inv = _neumann_inverse(_dot(kb, k, NT_DIMS) * decay)
            sol = _dot(inv, jnp.concatenate([v * beta, kb * jnp.exp(gcol)], axis=1))
            u_ref[0, d, pl.ds(r0, DN_PAIR), :] = sol[:, 0:DN_HEAD_DIM]
            w_ref[0, d, pl.ds(r0, DN_PAIR), :] = sol[:, DN_HEAD_DIM:2 * DN_HEAD_DIM]
        return carry

    lax.fori_loop(0, n_pairs, body, 0, unroll=2 if n_pairs % 2 == 0 else 1)


def _dn_intra(qkv, cols, grows, nb, rows):
    m = nb * rows
    n_pairs = rows // DN_PAIR
    steps = 1
    pps = n_pairs // steps
    blk = pps * DN_PAIR
    shape = jax.ShapeDtypeStruct((DN_HEADS, 2, m, LANES), F32)
    return pl.pallas_call(
        functools.partial(_dn_intra_kernel, n_pairs=pps),
        grid=(nb, DN_HEADS, steps),
        in_specs=[pl.BlockSpec((1, blk, LANES), lambda b, h, g: (DN_HEADS + h, b * steps + g, 0)),
                  pl.BlockSpec((1, blk, LANES), lambda b, h, g: (2 * DN_HEADS + h, b * steps + g, 0)),
                  pl.BlockSpec((1, 1, blk, 8), lambda b, h, g: (b, h, g, 0)),
                  pl.BlockSpec((1, 1, pps, 2, DN_PAIR), lambda b, h, g: (b, h, g, 0, 0))],
        out_specs=[pl.BlockSpec((1, 2, blk, LANES), lambda b, h, g: (h, 0, b * steps + g, 0)),
                   pl.BlockSpec((1, 2, blk, LANES), lambda b, h, g: (h, 0, b * steps + g, 0))],
        out_shape=[shape, shape],
        compiler_params=_cparams("arbitrary", "arbitrary", "arbitrary"),
        name="dn_intra",
    )(qkv, qkv, cols, grows)


def _dn_scan_kernel(q_ref, k_ref, u_ref, w_ref, col_ref, row_ref, z_ref, nw_ref, o_ref, acc_ref, *, n_chunks):
    masks = _chunk_masks(DN_CHUNK)
    acc_ref[...] = jnp.zeros_like(acc_ref)
    n_x = n_chunks[0]
    n_all = n_chunks[0] + n_chunks[1]

    def body(i, carry):
        states = []
        for d in range(2):
            s = carry[d]
            if d == 0:
                c = jnp.where(i < n_chunks[1], n_x + i, i - n_chunks[1])
            else:
                c = n_all - 1 - i
            r0 = pl.multiple_of(c * DN_CHUNK, DN_CHUNK)
            q = q_ref[0, pl.ds(r0, DN_CHUNK), :]
            k = k_ref[0, pl.ds(r0, DN_CHUNK), :]
            u = u_ref[0, d, pl.ds(r0, DN_CHUNK), :]
            w = w_ref[0, d, pl.ds(r0, DN_CHUNK), :]
            gcol = col_ref[0, 0, pl.ds(r0, DN_CHUNK), 2 + d:3 + d]
            grow = row_ref[0, 0, pl.ds(c, 1), d:d + 1, :][0]
            incl = masks[d][1]
            decay = jnp.exp(jnp.where(incl, gcol - grow, NEG_BIG))
            attn = _dot(q, k, NT_DIMS) * decay
            glast = gcol[DN_CHUNK - 1:DN_CHUNK, :] if d == 0 else gcol[0:1, :]
            qg = q * jnp.exp(gcol)
            kg = k * jnp.exp(glast - gcol)
            v_new = u - _dot(w, s)
            o = _dot(qg, s) + _dot(attn, v_new)
            acc_ref[pl.ds(r0, DN_CHUNK), :] += o
            states.append(s * jnp.exp(glast) + _dot(kg, v_new, TN_DIMS))
        return tuple(states)

    s0 = jnp.zeros((DN_HEAD_DIM, DN_HEAD_DIM), F32)
    lax.fori_loop(0, n_all, body, (s0, s0))
    o = acc_ref[...]
    y = o * lax.rsqrt(jnp.mean(o * o, axis=-1, keepdims=True) + NORM_EPS) * nw_ref[...]
    o_ref[0] = (y * _silu(z_ref[0].astype(F32))).astype(o_ref.dtype)


def _dn_scan(qkv, u, w, cols, grows, p, norm_w, nb, n_x, n_c, z_tile0):
    rows = n_x + n_c
    m = nb * rows
    n_chunks = (n_x // DN_CHUNK, n_c // DN_CHUNK)
    return pl.pallas_call(
        functools.partial(_dn_scan_kernel, n_chunks=n_chunks),
        grid=(nb, DN_HEADS),
        in_specs=[pl.BlockSpec((1, rows, LANES), lambda b, h: (h, b, 0)),
                  pl.BlockSpec((1, rows, LANES), lambda b, h: (DN_HEADS + h, b, 0)),
                  pl.BlockSpec((1, 2, rows, LANES), lambda b, h: (h, 0, b, 0)),
                  pl.BlockSpec((1, 2, rows, LANES), lambda b, h: (h, 0, b, 0)),
                  pl.BlockSpec((1, 1, rows, 8), lambda b, h: (b, h, 0, 0)),
                  pl.BlockSpec((1, 1, rows // DN_CHUNK, 2, DN_CHUNK), lambda b, h: (b, h, 0, 0, 0)),
                  pl.BlockSpec((1, rows, LANES), lambda b, h: (z_tile0 + h, b, 0)),
                  pl.BlockSpec((1, LANES), lambda b, h: (0, 0))],
        out_specs=pl.BlockSpec((1, rows, LANES), lambda b, h: (h, b, 0)),
        out_shape=jax.ShapeDtypeStruct((DN_HEADS, m, LANES), BF16),
        scratch_shapes=[pltpu.VMEM((rows, LANES), F32)],
        compiler_params=_cparams("arbitrary", "arbitrary"),
        name="dn_scan",
    )(qkv, qkv, u, w, cols, grows, p, norm_w.reshape(1, LANES).astype(F32))


def _deltanet_mixer(p, small, conv_w, a_log, dt_bias, norm_w, nb, n_x, n_c):
    rows = n_x + n_c
    qkv = _dn_conv(p, conv_w, nb, n_x, n_c)
    gates = _dn_gates(small, a_log, dt_bias).reshape(nb, rows, LANES)
    h = DN_HEADS
    cols = jnp.stack([gates[..., 0:h], gates[..., h:2 * h], gates[..., 4 * h:5 * h], gates[..., 5 * h:6 * h]],
                     axis=-1)
    cols = jnp.pad(cols.transpose(0, 2, 1, 3), ((0, 0), (0, 0), (0, 0), (0, 4)))
    grows = jnp.stack([gates[..., 4 * h:5 * h], gates[..., 5 * h:6 * h]], axis=-1)

    def row_form(width):
        return grows.reshape(nb, rows // width, width, h, 2).transpose(0, 3, 1, 4, 2)

    u, w = _dn_intra(qkv, cols, row_form(DN_PAIR), nb, rows)
    return _dn_scan(qkv, u, w, cols, row_form(DN_CHUNK), p, norm_w, nb, n_x, n_c, QKV_TILES)


def _gla_kernel(q_ref, k_ref, v_ref, r_ref, sm_ref, w2_ref, gb_ref, nw_ref, o_ref,
                gl_ref, acc_ref, st_ref, *, n_x, n_c):
    rows = n_x + n_c
    cs = GLA_CHUNK
    sm = sm_ref[...].astype(BF16)
    for d in range(2):
        z = jnp.dot(sm, w2_ref[d].astype(BF16), preferred_element_type=F32) + gb_ref[d]
        gl_ref[d] = (jnp.minimum(z, 0.0) - jnp.log1p(jnp.exp(-jnp.abs(z)))) * (1.0 / GLA_TAU)
    acc_ref[...] = jnp.zeros_like(acc_ref)
    st_ref[...] = jnp.zeros_like(st_ref)
    sub = GLA_SUB
    n_sub = cs // sub
    ri = lax.broadcasted_iota(jnp.int32, (cs, cs), 0)
    ci = lax.broadcasted_iota(jnp.int32, (cs, cs), 1)
    incl = ((ri >= ci).astype(F32), (ri <= ci).astype(F32))
    nxc, ncc = n_x // cs, n_c // cs
    scale = GLA_DK ** -0.5

    def sub_block(d, s):
        rs = slice(s * sub, (s + 1) * sub)
        if d == 0:
            return rs, slice(0, (s + 1) * sub), (s * sub - 1 if s > 0 else None)
        return rs, slice(s * sub, cs), ((s + 1) * sub if s < n_sub - 1 else None)

    def causal_mask(d, s):
        _, cols, _ = sub_block(d, s)
        width = cols.stop - cols.start
        row = lax.broadcasted_iota(jnp.int32, (sub, width), 0) + s * sub
        col = lax.broadcasted_iota(jnp.int32, (sub, width), 1) + cols.start
        return row >= col if d == 0 else row <= col

    causal = [[causal_mask(d, s) for s in range(n_sub)] for d in range(2)]

    def body(i, carry):
        for d in range(2):
            c = jnp.where(i < ncc, nxc + i, i - ncc) if d == 0 else nxc + ncc - 1 - i
            r0 = pl.multiple_of(c * cs, cs)
            g = gl_ref[d, pl.ds(r0, cs), :]
            cum = _tri_dot(incl[d], g)
            q = jnp.concatenate([q_ref[t, pl.ds(r0, cs), :] for t in range(2)], axis=1)
            k = jnp.concatenate([k_ref[t, pl.ds(r0, cs), :] for t in range(2)], axis=1)
            v = jnp.concatenate([v_ref[t, pl.ds(r0, cs), :] for t in range(4)], axis=1)
            clast = cum[cs - 1:cs, :] if d == 0 else cum[0:1, :]
            st = st_ref[d]
            o_inter = _dot(q * (jnp.exp(cum) * scale), st, NT_DIMS)
            parts = []
            for s in range(n_sub):
                rs, cols, ref_row = sub_block(d, s)
                ref = cum[ref_row:ref_row + 1, :] if ref_row is not None else 0.0
                qs = q[rs] * (jnp.exp(cum[rs] - ref) * scale)
                ks = k[cols] * jnp.exp(ref - cum[cols])
                a = jnp.where(causal[d][s], _dot(qs, ks, NT_DIMS), 0.0)
                parts.append(_dot(a, v[cols]))
            o = o_inter + jnp.concatenate(parts, axis=0)
            st_ref[d] = st * jnp.exp(clast) + _dot(v, k * jnp.exp(clast - cum), TN_DIMS)

            @pl.when(c < nxc)
            def _():
                acc_ref[pl.ds(r0, cs), :] += o
        return carry

    n_steps = rows // cs
    lax.fori_loop(0, n_steps, body, 0, unroll=2 if n_steps % 2 == 0 else 1)
    o = acc_ref[...]
    y = o * lax.rsqrt(jnp.mean(o * o, axis=-1, keepdims=True) + NORM_EPS) * nw_ref[...]
    for t in range(GLA_DV // LANES):
        r = r_ref[t, 0:n_x, :].astype(F32)
        o_ref[t] = (y[:, t * LANES:(t + 1) * LANES] * _silu(r)).astype(o_ref.dtype)


def _gla_mixer(p, small, gate_w2, gate_b, norm_w, nb, n_x, n_c):
    rows = n_x + n_c
    w2p = jnp.zeros((2, LANES, GLA_HEADS * GLA_DK), F32)
    for d in range(2):
        w2p = w2p.at[d, d * GLA_RANK:(d + 1) * GLA_RANK].set(gate_w2[d].astype(F32))
    qt, vt = GLA_DK // LANES, GLA_DV // LANES
    nq = GLA_HEADS
    return pl.pallas_call(
        functools.partial(_gla_kernel, n_x=n_x, n_c=n_c),
        grid=(nb, GLA_HEADS),
        in_specs=[pl.BlockSpec((qt, rows, LANES), lambda b, h: (h, b, 0), pipeline_mode=pl.Buffered(1)),
                  pl.BlockSpec((qt, rows, LANES), lambda b, h: (nq + h, b, 0), pipeline_mode=pl.Buffered(1)),
                  pl.BlockSpec((vt, rows, LANES), lambda b, h: (nq + h, b, 0), pipeline_mode=pl.Buffered(1)),
                  pl.BlockSpec((vt, rows, LANES), lambda b, h: (2 * nq + h, b, 0), pipeline_mode=pl.Buffered(1)),
                  pl.BlockSpec((rows, LANES), lambda b, h: (b, 0)),
                  pl.BlockSpec((2, LANES, GLA_DK), lambda b, h: (0, 0, h)),
                  pl.BlockSpec((2, 1, GLA_DK), lambda b, h: (0, 0, h)),
                  pl.BlockSpec((1, GLA_DV), lambda b, h: (0, 0))],
        out_specs=pl.BlockSpec((vt, n_x, LANES), lambda b, h: (h, b, 0)),
        out_shape=jax.ShapeDtypeStruct((GLA_HEADS * vt, nb * n_x, LANES), BF16),
        scratch_shapes=[pltpu.VMEM((2, rows, GLA_DK), F32),
                        pltpu.VMEM((n_x, GLA_DV), F32),
                        pltpu.VMEM((2, GLA_DV, GLA_DK), F32)],
        compiler_params=_cparams("arbitrary", "arbitrary"),
        name="gla",
    )(p, p, p, p, small, w2p, gate_b.astype(F32).reshape(2, 1, GLA_HEADS * GLA_DK),
      norm_w.astype(F32).reshape(1, GLA_DV))


def _router_kernel(h_ref, w_ref, e_ref, g_ref):
    logits = _dot(h_ref[...], w_ref[...])
    lane = lax.broadcasted_iota(jnp.int32, logits.shape, 1).astype(F32)
    logits = jnp.where(lane < N_EXPERTS, logits, NEG_BIG)
    m1 = jnp.max(logits, axis=-1, keepdims=True)
    i1 = jnp.min(jnp.where(logits == m1, lane, float(LANES)), axis=-1, keepdims=True)
    rest = jnp.where(lane == i1, NEG_BIG, logits)
    m2 = jnp.max(rest, axis=-1, keepdims=True)
    i2 = jnp.min(jnp.where(rest == m2, lane, float(LANES)), axis=-1, keepdims=True)
    e2 = jnp.exp(m2 - m1)
    den = 1.0 + e2
    e_ref[...] = jnp.where(lane == 0.0, i1, jnp.where(lane == 1.0, i2, 0.0)).astype(jnp.int32)
    g_ref[...] = jnp.where(lane == 0.0, 1.0 / den, jnp.where(lane == 1.0, e2 / den, 0.0))


def _router(h, w_router):
    m = h.shape[0]
    tm = 512
    wp = jnp.pad(w_router.astype(F32), ((0, 0), (0, LANES - N_EXPERTS)))
    return pl.pallas_call(
        _router_kernel,
        grid=(m // tm,),
        in_specs=[pl.BlockSpec((tm, D_MODEL), lambda i: (i, 0)),
                  pl.BlockSpec((D_MODEL, LANES), lambda i: (0, 0))],
        out_specs=[pl.BlockSpec((tm, LANES), lambda i: (i, 0)),
                   pl.BlockSpec((tm, LANES), lambda i: (i, 0))],
        out_shape=[jax.ShapeDtypeStruct((m, LANES), jnp.int32),
                   jax.ShapeDtypeStruct((m, LANES), F32)],
        compiler_params=_cparams("arbitrary"),
        name="router",
    )(h, wp)


def _row_copy(src_ref, src_row, dst_ref, dst_row, sem):
    return pltpu.make_async_copy(src_ref.at[pl.ds(src_row, 1)], dst_ref.at[pl.ds(dst_row, 1)], sem)


def _moe_kernel(be_ref, nused_ref, tok_ref, dst_ref, h_ref, w1_ref, w3_ref, w2_ref, y_ref,
                a32_ref, a16_ref, acc_ref, sem):
    i = pl.program_id(0)
    j = pl.program_id(1)
    nj = pl.num_programs(1)
    base = i * MOE_TM

    @pl.when(i < nused_ref[0])
    def _():
        @pl.when(j == 0)
        def _():
            def start(r, c):
                _row_copy(h_ref, tok_ref[base + r], a32_ref, r, sem).start()
                return c

            def wait(r, c):
                _row_copy(h_ref, tok_ref[base + r], a32_ref, r, sem).wait()
                return c

            lax.fori_loop(0, MOE_TM, start, 0, unroll=8)
            lax.fori_loop(0, MOE_TM, wait, 0, unroll=8)
            a16_ref[...] = a32_ref[...].astype(BF16)

        a = a16_ref[...]
        h1 = jnp.dot(a, w1_ref[0].astype(BF16), preferred_element_type=F32)
        h3 = jnp.dot(a, w3_ref[0].astype(BF16), preferred_element_type=F32)
        hh = (_silu(h1) * h3).astype(BF16)
        part = jnp.dot(hh, w2_ref[0].astype(BF16), preferred_element_type=F32)

        @pl.when(j == 0)
        def _():
            acc_ref[...] = part

        @pl.when(j > 0)
        def _():
            acc_ref[...] += part

        @pl.when(j == nj - 1)
        def _():
            def start(r, c):
                dst = dst_ref[base + r]

                @pl.when(dst >= 0)
                def _():
                    _row_copy(acc_ref, r, y_ref, dst, sem).start()
                return c

            def wait(r, c):
                dst = dst_ref[base + r]

                @pl.when(dst >= 0)
                def _():
                    _row_copy(acc_ref, r, y_ref, dst, sem).wait()
                return c

            lax.fori_loop(0, MOE_TM, start, 0, unroll=8)
            lax.fori_loop(0, MOE_TM, wait, 0, unroll=8)


def _moe(h, experts, w1, w3, w2):
    n_tok = h.shape[0]
    n_asg = n_tok * TOP_K
    n_blocks = -(-n_asg // MOE_TM) + N_EXPERTS
    n_rows = n_blocks * MOE_TM
    th = 256
    nj = EXPERT_HIDDEN // th
    flat_e = experts.reshape(n_asg)
    onehot = (flat_e[:, None] == jnp.arange(N_EXPERTS, dtype=jnp.int32)[None, :]).astype(jnp.int32)
    csum = jnp.cumsum(onehot, axis=0)
    rank = jnp.sum(csum * onehot, axis=1) - 1
    counts = csum[-1]
    padded = (counts + MOE_TM - 1) // MOE_TM * MOE_TM
    padded_end = jnp.cumsum(padded)
    padded_start = padded_end - padded
    dest = padded_start[flat_e] + rank
    asg = jnp.arange(n_asg, dtype=jnp.int32)
    row_tok = jnp.zeros((n_rows,), jnp.int32).at[dest].set(asg // TOP_K)
    row_dst = jnp.full((n_rows,), -1, jnp.int32).at[dest].set(asg)
    blk_start = jnp.arange(n_blocks, dtype=jnp.int32) * MOE_TM
    n_used = (padded_end[-1] // MOE_TM).astype(jnp.int32)
    blk_e = jnp.sum((padded_end[None, :] <= blk_start[:, None]).astype(jnp.int32), axis=1)
    last_e = jnp.sum((padded_end <= (n_used - 1) * MOE_TM).astype(jnp.int32))
    blk_e = jnp.where(blk_start < padded_end[-1], blk_e, last_e).astype(jnp.int32)

    def w_col(i, j, be, nu, *_):
        return jnp.where(i < nu[0], j, nj - 1)

    grid_spec = pltpu.PrefetchScalarGridSpec(
        num_scalar_prefetch=4,
        grid=(n_blocks, nj),
        in_specs=[pl.BlockSpec(memory_space=pl.ANY),
                  pl.BlockSpec((1, D_MODEL, th), lambda i, j, be, *s: (be[i], 0, w_col(i, j, be, *s))),
                  pl.BlockSpec((1, D_MODEL, th), lambda i, j, be, *s: (be[i], 0, w_col(i, j, be, *s))),
                  pl.BlockSpec((1, th, D_MODEL), lambda i, j, be, *s: (be[i], w_col(i, j, be, *s), 0))],
        out_specs=pl.BlockSpec(memory_space=pl.ANY),
        scratch_shapes=[pltpu.VMEM((MOE_TM, D_MODEL), F32),
                        pltpu.VMEM((MOE_TM, D_MODEL), BF16),
                        pltpu.VMEM((MOE_TM, D_MODEL), F32),
                        pltpu.SemaphoreType.DMA(())],
    )
    return pl.pallas_call(
        _moe_kernel,
        grid_spec=grid_spec,
        out_shape=jax.ShapeDtypeStruct((n_asg, D_MODEL), F32),
        compiler_params=_cparams("arbitrary", "arbitrary"),
        name="moe",
    )(blk_e, n_used.reshape(1), row_tok, row_dst, h, w1, w3, w2)


def kernel(x, c, ctx, c_ctx, ada_w, ada_b, ln1_g, ln1_b, ln2_g, ln2_b, e_w_in, s5_lam_re, s5_lam_im, s5_log_step, s5_b_re, s5_b_im, s5_c_re, s5_c_im, s5_d, dn_conv, dn_a_log, dn_dt_bias, dn_norm_w, e_w_out, ffn_w1, ffn_w3, ffn_w2, o_w_in, gla_w2, gla_b, gla_norm_w, o_w_out, moe_router, moe_w1, moe_w3, moe_w2):
    nb, n_x, _ = x.shape
    n_c = ctx.shape[1]
    rows = n_x + n_c
    m_all = nb * rows
    assert nb < 8 and n_x % (GRID_W * 8) == 0 and n_c % ROW_BLK == 0 and n_x % ROW_BLK == 0

    c8 = jnp.zeros((8, D_MODEL), F32).at[:nb].set(c).at[nb].set(c_ctx)
    mods = _adaln(c8, ada_w, ada_b).reshape(DEPTH, 8, N_MOD, D_MODEL)
    xc = jnp.concatenate([x, ctx], axis=1)

    h = _modulate(xc, mods, 0, n_x)
    p, small = _mm_in(h.reshape(m_all, D_MODEL), e_w_in[0])
    tables = _s5_tables(s5_lam_re[0].astype(F32), s5_lam_im[0].astype(F32), s5_log_step[0].astype(F32),
                        s5_b_re[0].astype(F32), s5_b_im[0].astype(F32), s5_c_re[0].astype(F32),
                        s5_c_im[0].astype(F32), s5_d[0])
    u_tile0 = (3 * DN_WIDTH + DN_WIDTH) // LANES
    y_s5 = _s5_mixer(p, tables, nb, n_x // S5_L, n_c // S5_L, u_tile0, u_tile0 + S5_TILES)
    y_dn = _deltanet_mixer(p, small, dn_conv[0], dn_a_log[0], dn_dt_bias[0], dn_norm_w[0], nb, n_x, n_c)
    mix = _mm_out([y_s5, y_dn], e_w_out[0]).reshape(nb, rows, D_MODEL)
    xc, h = _resid_ln(xc, mix, mods, 0, 2, 0, 3, ln1_g[0], ln1_b[0], n_x, rows, h_dtype=BF16)
    f = _ffn(h.reshape(m_all, D_MODEL), ffn_w1[0], ffn_w3[0], ffn_w2[0]).reshape(nb, rows, D_MODEL)
    xc, h = _resid_ln(xc, f, mods, 0, 5, 1, 0, ln2_g[0], ln2_b[0], n_x, rows, h_dtype=BF16)

    p, small = _mm_in(h.reshape(m_all, D_MODEL), o_w_in[0])
    y_gla = _gla_mixer(p, small, gla_w2[0], gla_b[0], gla_norm_w[0], nb, n_x, n_c)
    mix = _mm_out([y_gla], o_w_out[0]).reshape(nb, n_x, D_MODEL)
    xl, h = _resid_ln(xc, mix, mods, 1, 2, 1, 3, ln1_g[1], ln1_b[1], n_x, n_x, h_dtype=F32)
    hf = h.reshape(nb * n_x, D_MODEL)
    experts, gates = _router(hf, moe_router[0])
    y = _moe(hf, experts[:, :TOP_K], moe_w1[0], moe_w3[0], moe_w2[0])
    out, _ = _resid_ln(xl, y.reshape(nb, n_x, TOP_K * D_MODEL), mods, 1, 5, 1, 0, ln2_g[1], ln2_b[1], n_x, n_x,
                       moe_gates=gates[:, :TOP_K].reshape(nb, n_x, TOP_K))
    return out
```

```python
import functools
import math

import jax
import jax.numpy as jnp
from jax import lax
from jax.experimental import pallas as pl
from jax.experimental.pallas import tpu as pltpu

F32 = jnp.float32
BF16 = jnp.bfloat16

LANES = 128
VMEM_LIMIT = 60 * 1024 * 1024

D_MODEL = 2048
DEPTH = 2
ALPHA = (2 * DEPTH) ** 0.25
NORM_EPS = 1e-6
N_MOD = 6
GRID_W = 64

S5_WIDTH = D_MODEL // 2
S5_GROUP = 16
S5_GROUPS = S5_WIDTH // S5_GROUP
S5_STATE = 64
S5_L = 16
S5_TILES = S5_WIDTH // LANES
S5_GPT = LANES // S5_GROUP
S5_SW = S5_GPT * S5_STATE

DN_WIDTH = D_MODEL // 2
DN_HEAD_DIM = 128
DN_HEADS = DN_WIDTH // DN_HEAD_DIM
DN_CHUNK = 64
QKV_TILES = 3 * DN_WIDTH // LANES

GLA_HEADS = 4
GLA_DK = 256
GLA_DV = 512
GLA_RANK = 16
GLA_TAU = 16.0
GLA_CHUNK = 64
GLA_SUB = 16

MAIN_COLS = 6144
FFN_HIDDEN = 5632
N_EXPERTS = 8
TOP_K = 2
EXPERT_HIDDEN = 7168
MOE_TM = 1024
ROW_BLK = 256
NEG_BIG = -1e30

NT_DIMS = (((1,), (1,)), ((), ()))
TN_DIMS = (((0,), (0,)), ((), ()))


def _cparams(*sem):
    return pltpu.CompilerParams(dimension_semantics=sem, vmem_limit_bytes=VMEM_LIMIT)


def _dot(a, b, dims=None):
    a = a.astype(BF16)
    b = b.astype(BF16)
    if dims is None:
        return jnp.dot(a, b, preferred_element_type=F32)
    return lax.dot_general(a, b, dims, preferred_element_type=F32)


def _tri_dot(tri, g):
    g1 = g.astype(BF16)
    r1 = g - g1.astype(F32)
    g2 = r1.astype(BF16)
    g3 = (r1 - g2.astype(F32)).astype(BF16)
    t = tri.astype(BF16)
    return (jnp.dot(t, g1, preferred_element_type=F32) + jnp.dot(t, g2, preferred_element_type=F32)
            + jnp.dot(t, g3, preferred_element_type=F32))


def _silu(x):
    return x * jax.nn.sigmoid(x)


def _softplus(x):
    return jnp.maximum(x, 0.0) + jnp.log1p(jnp.exp(-jnp.abs(x)))


def _adaln_kernel(c_ref, w_ref, b_ref, o_ref):
    o_ref[0] = _dot(_silu(c_ref[...]), w_ref[0]) + b_ref[0]


def _adaln(c8, ada_w, ada_b):
    tn = 1024
    n = N_MOD * D_MODEL
    return pl.pallas_call(
        _adaln_kernel,
        grid=(DEPTH, n // tn),
        in_specs=[pl.BlockSpec((8, D_MODEL), lambda l, j: (0, 0)),
                  pl.BlockSpec((1, D_MODEL, tn), lambda l, j: (l, 0, j)),
                  pl.BlockSpec((1, 1, tn), lambda l, j: (l, 0, j))],
        out_specs=pl.BlockSpec((1, 8, tn), lambda l, j: (l, 0, j)),
        out_shape=jax.ShapeDtypeStruct((DEPTH, 8, n), F32),
        compiler_params=_cparams("arbitrary", "arbitrary"),
        name="adaln",
    )(c8, ada_w, ada_b.reshape(DEPTH, 1, n))


def _mod_row(nb, nx_blocks):
    return lambda b, r: jnp.where(r < nx_blocks, b, nb)


def _mod_kernel(x_ref, m_ref, o_ref):
    sh = m_ref[0, 0, 0:1, :]
    sc = m_ref[0, 0, 1:2, :]
    o_ref[0] = (x_ref[0] * (1.0 + sc) + sh).astype(o_ref.dtype)


def _modulate(xc, mods, layer, nx):
    nb, rows, _ = xc.shape
    row = _mod_row(nb, nx // ROW_BLK)
    return pl.pallas_call(
        _mod_kernel,
        grid=(nb, rows // ROW_BLK),
        in_specs=[pl.BlockSpec((1, ROW_BLK, D_MODEL), lambda b, r: (b, r, 0)),
                  pl.BlockSpec((1, 1, N_MOD, D_MODEL), lambda b, r: (layer, row(b, r), 0, 0))],
        out_specs=pl.BlockSpec((1, ROW_BLK, D_MODEL), lambda b, r: (b, r, 0)),
        out_shape=jax.ShapeDtypeStruct(xc.shape, BF16),
        compiler_params=_cparams("arbitrary", "arbitrary"),
        name="modulate",
    )(xc, mods)


def _ln_kernel(*refs, gate_row, moe, emit_h):
    it = iter(refs)
    x_ref, y_ref = next(it), next(it)
    gw_ref = next(it) if moe else None
    mg_ref, mn_ref, g_ref, b_ref, xo_ref = next(it), next(it), next(it), next(it), next(it)
    x = x_ref[0]
    if moe:
        gw = gw_ref[0]
        y = gw[:, 0:1] * y_ref[0, :, 0:D_MODEL] + gw[:, 1:2] * y_ref[0, :, D_MODEL:2 * D_MODEL]
    else:
        y = y_ref[0].astype(F32)
    v = ALPHA * x + mg_ref[0, 0, gate_row:gate_row + 1, :] * y
    mu = jnp.mean(v, axis=-1, keepdims=True)
    vc = v - mu
    var = jnp.mean(vc * vc, axis=-1, keepdims=True)
    xn = vc * lax.rsqrt(var + NORM_EPS) * g_ref[...] + b_ref[...]
    xo_ref[0] = xn
    if emit_h:
        ho_ref = next(it)
        sh = mn_ref[0, 0, 0:1, :]
        sc = mn_ref[0, 0, 1:2, :]
        ho_ref[0] = (xn * (1.0 + sc) + sh).astype(ho_ref.dtype)


def _resid_ln(xc, y, mods, gate_layer, gate_row, next_layer, next_row0, ln_g, ln_b, nx, rows_out,
              h_dtype=None, moe_gates=None):
    nb = xc.shape[0]
    row = _mod_row(nb, nx // ROW_BLK)
    moe = moe_gates is not None
    emit_h = h_dtype is not None
    mods_next = mods[:, :, next_row0:next_row0 + 2, :] if emit_h else mods[:, :, 0:2, :]
    in_specs = [pl.BlockSpec((1, ROW_BLK, D_MODEL), lambda b, r: (b, r, 0)),
                pl.BlockSpec((1, ROW_BLK, y.shape[-1]), lambda b, r: (b, r, 0))]
    args = [xc, y]
    if moe:
        in_specs.append(pl.BlockSpec((1, ROW_BLK, TOP_K), lambda b, r: (b, r, 0)))
        args.append(moe_gates)
    in_specs += [pl.BlockSpec((1, 1, N_MOD, D_MODEL), lambda b, r: (gate_layer, row(b, r), 0, 0)),
                 pl.BlockSpec((1, 1, 2, D_MODEL), lambda b, r: (next_layer, row(b, r), 0, 0)),
                 pl.BlockSpec((1, D_MODEL), lambda b, r: (0, 0)),
                 pl.BlockSpec((1, D_MODEL), lambda b, r: (0, 0))]
    args += [mods, mods_next, ln_g.reshape(1, D_MODEL), ln_b.reshape(1, D_MODEL)]
    out_shape = [jax.ShapeDtypeStruct((nb, rows_out, D_MODEL), F32)]
    out_specs = [pl.BlockSpec((1, ROW_BLK, D_MODEL), lambda b, r: (b, r, 0))]
    if emit_h:
        out_shape.append(jax.ShapeDtypeStruct((nb, rows_out, D_MODEL), h_dtype))
        out_specs.append(pl.BlockSpec((1, ROW_BLK, D_MODEL), lambda b, r: (b, r, 0)))
    res = pl.pallas_call(
        functools.partial(_ln_kernel, gate_row=gate_row, moe=moe, emit_h=emit_h),
        grid=(nb, rows_out // ROW_BLK),
        in_specs=in_specs, out_specs=out_specs, out_shape=out_shape,
        compiler_params=_cparams("arbitrary", "arbitrary"),
        name="resid_ln",
    )(*args)
    return res if emit_h else (res[0], None)


def _mm_in_kernel(a_ref, w_ref, ws_ref, o_ref, os_ref, *, tn):
    a = a_ref[...]
    res = jnp.dot(a, w_ref[...].astype(BF16), preferred_element_type=F32)
    for t in range(tn // LANES):
        o_ref[t] = res[:, t * LANES:(t + 1) * LANES].astype(o_ref.dtype)

    @pl.when(pl.program_id(1) == 0)
    def _():
        os_ref[...] = jnp.dot(a, ws_ref[...].astype(BF16), preferred_element_type=F32)


def _row_tile(m, cap):
    best = 16
    for t in range(16, cap + 1, 16):
        if m % t == 0:
            best = t
    return best


def _mm_in(a, w):
    m = a.shape[0]
    tm = _row_tile(m, 1088)
    tn = 512
    w_small = jnp.pad(w[:, MAIN_COLS:], ((0, 0), (0, LANES - (w.shape[1] - MAIN_COLS))))
    return pl.pallas_call(
        functools.partial(_mm_in_kernel, tn=tn),
        grid=(m // tm, MAIN_COLS // tn),
        in_specs=[pl.BlockSpec((tm, D_MODEL), lambda i, j: (i, 0)),
                  pl.BlockSpec((D_MODEL, tn), lambda i, j: (0, j)),
                  pl.BlockSpec((D_MODEL, LANES), lambda i, j: (0, 0))],
        out_specs=[pl.BlockSpec((tn // LANES, tm, LANES), lambda i, j: (j, i, 0)),
                   pl.BlockSpec((tm, LANES), lambda i, j: (i, 0))],
        out_shape=[jax.ShapeDtypeStruct((MAIN_COLS // LANES, m, LANES), BF16),
                   jax.ShapeDtypeStruct((m, LANES), F32)],
        compiler_params=_cparams("arbitrary", "arbitrary"),
        name="mm_in",
    )(a, w, w_small)


def _mm_out_kernel(*refs, n_a):
    a_refs, w_ref, o_ref = refs[:n_a], refs[n_a], refs[n_a + 1]
    a = jnp.concatenate([r[t] for r in a_refs for t in range(r.shape[0])], axis=1)
    o_ref[...] = jnp.dot(a, w_ref[...].astype(BF16), preferred_element_type=F32)


def _mm_out(a_list, w):
    m = a_list[0].shape[1]
    tm = _row_tile(m, 1088)
    tn = 512
    in_specs = [pl.BlockSpec((a.shape[0], tm, LANES), lambda i, j: (0, i, 0)) for a in a_list]
    in_specs.append(pl.BlockSpec((D_MODEL, tn), lambda i, j: (0, j)))
    return pl.pallas_call(
        functools.partial(_mm_out_kernel, n_a=len(a_list)),
        grid=(m // tm, D_MODEL // tn),
        in_specs=in_specs,
        out_specs=pl.BlockSpec((tm, tn), lambda i, j: (i, j)),
        out_shape=jax.ShapeDtypeStruct((m, D_MODEL), F32),
        compiler_params=_cparams("arbitrary", "arbitrary"),
        name="mm_out",
    )(*a_list, w)


ACC_TN = 512


def _accumulate(acc_ref, hh, w2):
    for n0 in range(0, acc_ref.shape[1], ACC_TN):
        acc_ref[:, n0:n0 + ACC_TN] += jnp.dot(hh, w2[:, n0:n0 + ACC_TN], preferred_element_type=F32)


def _ffn_kernel(a_ref, w1_ref, w3_ref, w2_ref, o_ref):
    j = pl.program_id(1)
    a = a_ref[...]
    h1 = jnp.dot(a, w1_ref[...].astype(BF16), preferred_element_type=F32)
    h3 = jnp.dot(a, w3_ref[...].astype(BF16), preferred_element_type=F32)
    hh = (_silu(h1) * h3).astype(BF16)

    @pl.when(j == 0)
    def _():
        o_ref[...] = jnp.zeros_like(o_ref)

    _accumulate(o_ref, hh, w2_ref[...].astype(BF16))


def _ffn(a, w1, w3, w2):
    m = a.shape[0]
    hidden = w1.shape[1]
    tm = _row_tile(m, 1088)
    th = 256
    return pl.pallas_call(
        _ffn_kernel,
        grid=(m // tm, hidden // th),
        in_specs=[pl.BlockSpec((tm, D_MODEL), lambda i, j: (i, 0)),
                  pl.BlockSpec((D_MODEL, th), lambda i, j: (0, j)),
                  pl.BlockSpec((D_MODEL, th), lambda i, j: (0, j)),
                  pl.BlockSpec((th, D_MODEL), lambda i, j: (j, 0))],
        out_specs=pl.BlockSpec((tm, D_MODEL), lambda i, j: (i, 0)),
        out_shape=jax.ShapeDtypeStruct((m, D_MODEL), F32),
        compiler_params=_cparams("arbitrary", "arbitrary"),
        name="ffn",
    )(a, w1, w3, w2)


def _s5_tables(lam_re, lam_im, log_step, b_re, b_im, c_re, c_im, d_skip):
    hp = lax.Precision.HIGHEST
    tau = jnp.arange(S5_L + 1, dtype=F32)[:, None, None]
    lag_k, ab, ca, a_l = [], [], [], []
    for d in range(2):
        delta = jnp.exp(log_step[d])[:, None]
        lr, li = lam_re[d], lam_im[d]
        mag = jnp.exp(lr * delta * tau)
        ang = li * delta * tau
        are, aim = mag * jnp.cos(ang), mag * jnp.sin(ang)
        den = jnp.square(lr) + jnp.square(li)
        f_re = ((are[1] - 1.0) * lr + aim[1] * li) / den
        f_im = (aim[1] * lr - (are[1] - 1.0) * li) / den
        bbr = f_re[..., None] * b_re[d] - f_im[..., None] * b_im[d]
        bbi = f_re[..., None] * b_im[d] + f_im[..., None] * b_re[d]
        car = c_re[d][None] * are[:, :, None, :] - c_im[d][None] * aim[:, :, None, :]
        cai = c_re[d][None] * aim[:, :, None, :] + c_im[d][None] * are[:, :, None, :]
        lag_k.append(jnp.einsum('tghp,gpk->tghk', car, bbr, precision=hp)
                     - jnp.einsum('tghp,gpk->tghk', cai, bbi, precision=hp))
        ab.append((are[..., None] * bbr[None] - aim[..., None] * bbi[None],
                   are[..., None] * bbi[None] + aim[..., None] * bbr[None]))
        ca.append((car, cai))
        a_l.append((are[S5_L], aim[S5_L]))
    eye = jnp.eye(S5_GPT, dtype=F32)
    kf, kb = lag_k
    lag = jnp.concatenate([kb[S5_L - 1:0:-1], (kf[0] + kb[0])[None], kf[1:S5_L]], axis=0)
    lag = lag.reshape(2 * S5_L - 1, S5_TILES, S5_GPT, S5_GROUP, S5_GROUP)
    wlag = jnp.einsum('ljghk,gf->jgklfh', lag, eye).reshape(S5_TILES, LANES, (2 * S5_L - 1) * LANES)
    wb = jnp.stack([ab[0][0][:S5_L][::-1], ab[0][1][:S5_L][::-1], ab[1][0][:S5_L], ab[1][1][:S5_L]], axis=0)
    wb = wb.reshape(4, S5_L, S5_TILES, S5_GPT, S5_STATE, S5_GROUP)
    wb = jnp.einsum('csjgpk,gf->jsgkcfp', wb, eye).reshape(S5_TILES, S5_L * LANES, 4 * S5_SW)
    wc = jnp.stack([ca[0][0][1:], -ca[0][1][1:], ca[1][0][S5_L:0:-1], -ca[1][1][S5_L:0:-1]], axis=0)
    wc = wc.reshape(4, S5_L, S5_TILES, S5_GPT, S5_GROUP, S5_STATE)
    wc = jnp.einsum('ctjghp,gf->jcfptgh', wc, eye).reshape(S5_TILES, 4 * S5_SW, S5_L * LANES)
    a16 = jnp.stack([a_l[0][0], a_l[0][1], a_l[1][0], a_l[1][1]], axis=0)
    a16 = a16.reshape(4, S5_TILES, S5_SW).transpose(1, 0, 2)
    dsk = jnp.tile(d_skip.astype(F32).reshape(S5_TILES, 1, LANES), (1, 1, S5_L))
    return wlag.astype(BF16), wb.astype(BF16), wc.astype(BF16), a16, dsk


def _s5_kernel(u_ref, g_ref, wlag_ref, wb_ref, wc_ref, a16_ref, d_ref, o_ref,
               wt_ref, s_ref, hin_ref, *, n_x, n_c):
    rows = n_x + n_c

    @pl.when(pl.program_id(1) == 0)
    def _():
        for s in range(S5_L):
            wt_ref[s * LANES:(s + 1) * LANES, :] = wlag_ref[0, :, (S5_L - 1 - s) * LANES:(2 * S5_L - 1 - s) * LANES]

    ub = u_ref[0]
    u = ub.astype(F32)
    s_ref[...] = jnp.dot(ub, wb_ref[0], preferred_element_type=F32)
    a = a16_ref[0]
    afr, afi, abr, abi = a[0:1], a[1:2], a[2:3], a[3:4]
    w = S5_SW

    sub = 8
    tiles_x, tiles_c = n_x // sub, n_c // sub

    def tile_scan(t, hr, hi, ar, ai, lane0, forward):
        r0 = pl.multiple_of(t * sub, sub)
        sr = s_ref[pl.ds(r0, sub), lane0:lane0 + w]
        si = s_ref[pl.ds(r0, sub), lane0 + w:lane0 + 2 * w]
        in_r, in_i = [None] * sub, [None] * sub
        for r in (range(sub) if forward else range(sub - 1, -1, -1)):
            in_r[r], in_i[r] = hr, hi
            hr, hi = ar * hr - ai * hi + sr[r:r + 1], ar * hi + ai * hr + si[r:r + 1]
        hin_ref[pl.ds(r0, sub), lane0:lane0 + w] = jnp.concatenate(in_r, axis=0)
        hin_ref[pl.ds(r0, sub), lane0 + w:lane0 + 2 * w] = jnp.concatenate(in_i, axis=0)
        return hr, hi

    def body(i, carry):
        hfr, hfi, hbr, hbi = carry
        tf = jnp.where(i < tiles_c, tiles_x + i, i - tiles_c)
        tb = tiles_x + tiles_c - 1 - i
        hfr, hfi = tile_scan(tf, hfr, hfi, afr, afi, 0, True)
        hbr, hbi = tile_scan(tb, hbr, hbi, abr, abi, 2 * w, False)
        return hfr, hfi, hbr, hbi

    z = jnp.zeros((1, w), F32)
    lax.fori_loop(0, tiles_x + tiles_c, body, (z, z, z, z))
    y = (jnp.dot(ub, wt_ref[...], preferred_element_type=F32)
         + jnp.dot(hin_ref[...].astype(BF16), wc_ref[0], preferred_element_type=F32))
    y = y + d_ref[0] * u
    gelu = 0.5 * y * (1.0 + jnp.tanh(math.sqrt(2.0 / math.pi) * (y + 0.044715 * (y * y * y))))
    o_ref[0] = (gelu * jax.nn.sigmoid(g_ref[0].astype(F32))).astype(o_ref.dtype)


def _s5_mixer(p, tables, nb, n_x, n_c, u_tile0, g_tile0):
    wlag, wb, wc, a16, dsk = tables
    rows = n_x + n_c
    width = S5_L * LANES
    pv = p.reshape(p.shape[0], nb * rows, width)
    return pl.pallas_call(
        functools.partial(_s5_kernel, n_x=n_x, n_c=n_c),
        grid=(S5_TILES, nb),
        in_specs=[pl.BlockSpec((1, rows, width), lambda j, b: (u_tile0 + j, b, 0)),
                  pl.BlockSpec((1, rows, width), lambda j, b: (g_tile0 + j, b, 0)),
                  pl.BlockSpec((1, LANES, wlag.shape[2]), lambda j, b: (j, 0, 0), pipeline_mode=pl.Buffered(1)),
                  pl.BlockSpec((1, width, 4 * S5_SW), lambda j, b: (j, 0, 0), pipeline_mode=pl.Buffered(1)),
                  pl.BlockSpec((1, 4 * S5_SW, width), lambda j, b: (j, 0, 0), pipeline_mode=pl.Buffered(1)),
                  pl.BlockSpec((1, 4, S5_SW), lambda j, b: (j, 0, 0)),
                  pl.BlockSpec((1, 1, width), lambda j, b: (j, 0, 0))],
        out_specs=pl.BlockSpec((1, rows, width), lambda j, b: (j, b, 0)),
        out_shape=jax.ShapeDtypeStruct((S5_TILES, nb * rows, width), BF16),
        scratch_shapes=[pltpu.VMEM((width, width), BF16),
                        pltpu.VMEM((rows, 4 * S5_SW), F32),
                        pltpu.VMEM((rows, 4 * S5_SW), F32)],
        compiler_params=_cparams("arbitrary", "arbitrary"),
        name="s5",
    )(pv, pv, wlag, wb, wc, a16, dsk).reshape(S5_TILES, nb * rows * S5_L, LANES)


def _dn_conv_kernel(x_ref, w_ref, o_ref, xp_ref, cp_ref, *, n_x, n_c):
    pad = 72
    blk = 512
    tile = pl.program_id(0)
    w = w_ref[...]
    xp_ref[0:pad, :] = jnp.zeros((pad, LANES), F32)
    xp_ref[pad + n_x:pad + n_x + pad, :] = jnp.zeros((pad, LANES), F32)
    xp_ref[pad:pad + n_x, :] = x_ref[0, 0:n_x, :].astype(F32)
    cp_ref[0:8, :] = jnp.zeros((8, LANES), F32)
    cp_ref[8 + n_c:16 + n_c, :] = jnp.zeros((8, LANES), F32)
    cp_ref[8:8 + n_c, :] = x_ref[0, n_x:n_x + n_c, :].astype(F32)
    is_qk = tile < 2 * DN_HEADS
    scale = jnp.where(tile < DN_HEADS, DN_HEAD_DIM ** -0.5, 1.0)

    def finish(y):
        y = _silu(y)
        nrm = y * lax.rsqrt(jnp.sum(y * y, axis=-1, keepdims=True) + NORM_EPS) * scale
        return jnp.where(is_qk, nrm, y)

    col = lax.broadcasted_iota(jnp.int32, (blk, LANES), 0) & (GRID_W - 1)
    not_first = col != 0
    not_last = col != GRID_W - 1
    for r0 in range(0, n_x, blk):
        left = mid = right = None
        for kh in range(3):
            base = pad + r0 + (kh - 1) * GRID_W
            tl = xp_ref[base - 1:base - 1 + blk, :] * w[3 * kh:3 * kh + 1, :]
            tm = xp_ref[base:base + blk, :] * w[3 * kh + 1:3 * kh + 2, :]
            tr = xp_ref[base + 1:base + 1 + blk, :] * w[3 * kh + 2:3 * kh + 3, :]
            left = tl if left is None else left + tl
            mid = tm if mid is None else mid + tm
            right = tr if right is None else right + tr
        y = jnp.where(not_first, left, 0.0) + mid + jnp.where(not_last, right, 0.0)
        o_ref[0, r0:r0 + blk, :] = finish(y)
    yc = (cp_ref[7:7 + n_c, :] * w[3:4, :] + cp_ref[8:8 + n_c, :] * w[4:5, :]
          + cp_ref[9:9 + n_c, :] * w[5:6, :])
    o_ref[0, n_x:n_x + n_c, :] = finish(yc)


def _dn_conv(p, conv_w, nb, n_x, n_c):
    rows = n_x + n_c
    w9 = conv_w.reshape(9, 3 * DN_WIDTH)
    return pl.pallas_call(
        functools.partial(_dn_conv_kernel, n_x=n_x, n_c=n_c),
        grid=(QKV_TILES, nb),
        in_specs=[pl.BlockSpec((1, rows, LANES), lambda t, b: (t, b, 0)),
                  pl.BlockSpec((9, LANES), lambda t, b: (0, t))],
        out_specs=pl.BlockSpec((1, rows, LANES), lambda t, b: (t, b, 0)),
        out_shape=jax.ShapeDtypeStruct((QKV_TILES, nb * rows, LANES), F32),
        scratch_shapes=[pltpu.VMEM((n_x + 144, LANES), F32), pltpu.VMEM((n_c + 16, LANES), F32)],
        compiler_params=_cparams("arbitrary", "arbitrary"),
        name="dn_conv",
    )(p, w9)


def _dn_gates_kernel(s_ref, coef_ref, dtb_ref, o_ref, *, tm):
    s = s_ref[...]
    lane = lax.broadcasted_iota(jnp.int32, (tm, LANES), 1)
    beta = jax.nn.sigmoid(s)
    g = coef_ref[...] * _softplus(s + dtb_ref[...])
    o_ref[...] = jnp.where(lane < 2 * DN_HEADS, beta, g)
    ri = lax.broadcasted_iota(jnp.int32, (DN_CHUNK, DN_CHUNK), 0)
    ci = lax.broadcasted_iota(jnp.int32, (DN_CHUNK, DN_CHUNK), 1)
    lower = (ri >= ci).astype(F32)
    upper = (ri <= ci).astype(F32)
    lane_c = lax.broadcasted_iota(jnp.int32, (DN_CHUNK, LANES), 1)
    for c0 in range(0, tm, DN_CHUNK):
        gc = g[c0:c0 + DN_CHUNK, :]
        gsh = pltpu.roll(gc, 2 * DN_HEADS, 1)
        cf = _tri_dot(lower, gsh)
        cb = _tri_dot(upper, gsh)
        cum = jnp.where(lane_c < 5 * DN_HEADS, cf, cb)
        cur = o_ref[c0:c0 + DN_CHUNK, :]
        o_ref[c0:c0 + DN_CHUNK, :] = jnp.where((lane_c >= 4 * DN_HEADS) & (lane_c < 6 * DN_HEADS), cum, cur)


def _dn_gates(small, a_log, dt_bias):
    m = small.shape[0]
    tm = _row_tile(m, 1088)
    coef = jnp.zeros((LANES,), F32).at[2 * DN_HEADS:4 * DN_HEADS].set(-jnp.exp(a_log.astype(F32)).reshape(-1))
    dtb = jnp.zeros((LANES,), F32).at[2 * DN_HEADS:4 * DN_HEADS].set(dt_bias.astype(F32).reshape(-1))
    return pl.pallas_call(
        functools.partial(_dn_gates_kernel, tm=tm),
        grid=(m // tm,),
        in_specs=[pl.BlockSpec((tm, LANES), lambda i: (i, 0)),
                  pl.BlockSpec((1, LANES), lambda i: (0, 0)),
                  pl.BlockSpec((1, LANES), lambda i: (0, 0))],
        out_specs=pl.BlockSpec((tm, LANES), lambda i: (i, 0)),
        out_shape=jax.ShapeDtypeStruct((m, LANES), F32),
        compiler_params=_cparams("arbitrary"),
        name="dn_gates",
    )(small, coef.reshape(1, LANES), dtb.reshape(1, LANES))


def _neumann_inverse(lm):
    n = lm.shape[0]
    eye = (lax.broadcasted_iota(jnp.int32, (n, n), 0) == lax.broadcasted_iota(jnp.int32, (n, n), 1)).astype(F32)
    inv = eye - lm
    pw = _dot(lm, lm)
    for _ in range(4):
        inv = inv + _dot(inv, pw)
        pw = _dot(pw, pw)
    return inv + _dot(inv, pw)


def _chunk_masks(n):
    ri = lax.broadcasted_iota(jnp.int32, (n, n), 0)
    ci = lax.broadcasted_iota(jnp.int32, (n, n), 1)
    return ((ri > ci, ri >= ci), (ri < ci, ri <= ci))


DN_PAIR = 2 * DN_CHUNK


def _dn_intra_kernel(k_ref, v_ref, col_ref, row_ref, u_ref, w_ref, *, n_pairs):
    ri = lax.broadcasted_iota(jnp.int32, (DN_PAIR, DN_PAIR), 0)
    ci = lax.broadcasted_iota(jnp.int32, (DN_PAIR, DN_PAIR), 1)
    shift = DN_CHUNK.bit_length() - 1
    same = (ri >> shift) == (ci >> shift)
    strict = (same & (ri > ci), same & (ri < ci))

    def body(g, carry):
        r0 = pl.multiple_of(g * DN_PAIR, DN_PAIR)
        k = k_ref[0, pl.ds(r0, DN_PAIR), :]
        v = v_ref[0, pl.ds(r0, DN_PAIR), :]
        for d in range(2):
            beta = col_ref[0, 0, pl.ds(r0, DN_PAIR), d:d + 1]
            gcol = col_ref[0, 0, pl.ds(r0, DN_PAIR), 2 + d:3 + d]
            grow = row_ref[0, 0, pl.ds(g, 1), d:d + 1, :][0]
            decay = jnp.exp(jnp.where(strict[d], gcol - grow, NEG_BIG))
            kb = k * beta
            inv = _neumann_inverse(_dot(kb, k, NT_DIMS) * decay)
            sol = _dot(inv, jnp.concatenate([v * beta, kb * jnp.exp(gcol)], axis=1))
            u_ref[0, d, pl.ds(r0, DN_PAIR), :] = sol[:, 0:DN_HEAD_DIM]
            w_ref[0, d, pl.ds(r0, DN_PAIR), :] = sol[:, DN_HEAD_DIM:2 * DN_HEAD_DIM]
        return carry

    lax.fori_loop(0, n_pairs, body, 0, unroll=2 if n_pairs % 2 == 0 else 1)


def _dn_intra(qkv, cols, grows, nb, rows):
    m = nb * rows
    n_pairs = rows // DN_PAIR
    steps = 1
    pps = n_pairs // steps
    blk = pps * DN_PAIR
    shape = jax.ShapeDtypeStruct((DN_HEADS, 2, m, LANES), F32)
    return pl.pallas_call(
        functools.partial(_dn_intra_kernel, n_pairs=pps),
        grid=(nb, DN_HEADS, steps),
        in_specs=[pl.BlockSpec((1, blk, LANES), lambda b, h, g: (DN_HEADS + h, b * steps + g, 0)),
                  pl.BlockSpec((1, blk, LANES), lambda b, h, g: (2 * DN_HEADS + h, b * steps + g, 0)),
                  pl.BlockSpec((1, 1, blk, 8), lambda b, h, g: (b, h, g, 0)),
                  pl.BlockSpec((1, 1, pps, 2, DN_PAIR), lambda b, h, g: (b, h, g, 0, 0))],
        out_specs=[pl.BlockSpec((1, 2, blk, LANES), lambda b, h, g: (h, 0, b * steps + g, 0)),
                   pl.BlockSpec((1, 2, blk, LANES), lambda b, h, g: (h, 0, b * steps + g, 0))],
        out_shape=[shape, shape],
        compiler_params=_cparams("arbitrary", "arbitrary", "arbitrary"),
        name="dn_intra",
    )(qkv, qkv, cols, grows)


def _dn_scan_kernel(q_ref, k_ref, u_ref, w_ref, col_ref, row_ref, z_ref, nw_ref, o_ref, acc_ref, *, n_chunks):
    masks = _chunk_masks(DN_CHUNK)
    acc_ref[...] = jnp.zeros_like(acc_ref)
    n_x = n_chunks[0]
    n_all = n_chunks[0] + n_chunks[1]

    def body(i, carry):
        states = []
        for d in range(2):
            s = carry[d]
            if d == 0:
                c = jnp.where(i < n_chunks[1], n_x + i, i - n_chunks[1])
            else:
                c = n_all - 1 - i
            r0 = pl.multiple_of(c * DN_CHUNK, DN_CHUNK)
            q = q_ref[0, pl.ds(r0, DN_CHUNK), :]
            k = k_ref[0, pl.ds(r0, DN_CHUNK), :]
            u = u_ref[0, d, pl.ds(r0, DN_CHUNK), :]
            w = w_ref[0, d, pl.ds(r0, DN_CHUNK), :]
            gcol = col_ref[0, 0, pl.ds(r0, DN_CHUNK), 2 + d:3 + d]
            grow = row_ref[0, 0, pl.ds(c, 1), d:d + 1, :][0]
            incl = masks[d][1]
            decay = jnp.exp(jnp.where(incl, gcol - grow, NEG_BIG))
            attn = _dot(q, k, NT_DIMS) * decay
            glast = gcol[DN_CHUNK - 1:DN_CHUNK, :] if d == 0 else gcol[0:1, :]
            qg = q * jnp.exp(gcol)
            kg = k * jnp.exp(glast - gcol)
            v_new = u - _dot(w, s)
            o = _dot(qg, s) + _dot(attn, v_new)
            acc_ref[pl.ds(r0, DN_CHUNK), :] += o
            states.append(s * jnp.exp(glast) + _dot(kg, v_new, TN_DIMS))
        return tuple(states)

    s0 = jnp.zeros((DN_HEAD_DIM, DN_HEAD_DIM), F32)
    lax.fori_loop(0, n_all, body, (s0, s0))
    o = acc_ref[...]
    y = o * lax.rsqrt(jnp.mean(o * o, axis=-1, keepdims=True) + NORM_EPS) * nw_ref[...]
    o_ref[0] = (y * _silu(z_ref[0].astype(F32))).astype(o_ref.dtype)


def _dn_scan(qkv, u, w, cols, grows, p, norm_w, nb, n_x, n_c, z_tile0):
    rows = n_x + n_c
    m = nb * rows
    n_chunks = (n_x // DN_CHUNK, n_c // DN_CHUNK)
    return pl.pallas_call(
        functools.partial(_dn_scan_kernel, n_chunks=n_chunks),
        grid=(nb, DN_HEADS),
        in_specs=[pl.BlockSpec((1, rows, LANES), lambda b, h: (h, b, 0)),
                  pl.BlockSpec((1, rows, LANES), lambda b, h: (DN_HEADS + h, b, 0)),
                  pl.BlockSpec((1, 2, rows, LANES), lambda b, h: (h, 0, b, 0)),
                  pl.BlockSpec((1, 2, rows, LANES), lambda b, h: (h, 0, b, 0)),
                  pl.BlockSpec((1, 1, rows, 8), lambda b, h: (b, h, 0, 0)),
                  pl.BlockSpec((1, 1, rows // DN_CHUNK, 2, DN_CHUNK), lambda b, h: (b, h, 0, 0, 0)),
                  pl.BlockSpec((1, rows, LANES), lambda b, h: (z_tile0 + h, b, 0)),
                  pl.BlockSpec((1, LANES), lambda b, h: (0, 0))],
        out_specs=pl.BlockSpec((1, rows, LANES), lambda b, h: (h, b, 0)),
        out_shape=jax.ShapeDtypeStruct((DN_HEADS, m, LANES), BF16),
        scratch_shapes=[pltpu.VMEM((rows, LANES), F32)],
        compiler_params=_cparams("arbitrary", "arbitrary"),
        name="dn_scan",
    )(qkv, qkv, u, w, cols, grows, p, norm_w.reshape(1, LANES).astype(F32))


def _deltanet_mixer(p, small, conv_w, a_log, dt_bias, norm_w, nb, n_x, n_c):
    rows = n_x + n_c
    qkv = _dn_conv(p, conv_w, nb, n_x, n_c)
    gates = _dn_gates(small, a_log, dt_bias).reshape(nb, rows, LANES)
    h = DN_HEADS
    cols = jnp.stack([gates[..., 0:h], gates[..., h:2 * h], gates[..., 4 * h:5 * h], gates[..., 5 * h:6 * h]],
                     axis=-1)
    cols = jnp.pad(cols.transpose(0, 2, 1, 3), ((0, 0), (0, 0), (0, 0), (0, 4)))
    grows = jnp.stack([gates[..., 4 * h:5 * h], gates[..., 5 * h:6 * h]], axis=-1)

    def row_form(width):
        return grows.reshape(nb, rows // width, width, h, 2).transpose(0, 3, 1, 4, 2)

    u, w = _dn_intra(qkv, cols, row_form(DN_PAIR), nb, rows)
    return _dn_scan(qkv, u, w, cols, row_form(DN_CHUNK), p, norm_w, nb, n_x, n_c, QKV_TILES)


def _gla_kernel(q_ref, k_ref, v_ref, r_ref, sm_ref, w2_ref, gb_ref, nw_ref, o_ref,
                gl_ref, acc_ref, st_ref, *, n_x, n_c):
    rows = n_x + n_c
    cs = GLA_CHUNK
    sm = sm_ref[...].astype(BF16)
    for d in range(2):
        z = jnp.dot(sm, w2_ref[d].astype(BF16), preferred_element_type=F32) + gb_ref[d]
        gl_ref[d] = (jnp.minimum(z, 0.0) - jnp.log1p(jnp.exp(-jnp.abs(z)))) * (1.0 / GLA_TAU)
    acc_ref[...] = jnp.zeros_like(acc_ref)
    st_ref[...] = jnp.zeros_like(st_ref)
    sub = GLA_SUB
    n_sub = cs // sub
    ri = lax.broadcasted_iota(jnp.int32, (cs, cs), 0)
    ci = lax.broadcasted_iota(jnp.int32, (cs, cs), 1)
    incl = ((ri >= ci).astype(F32), (ri <= ci).astype(F32))
    nxc, ncc = n_x // cs, n_c // cs
    scale = GLA_DK ** -0.5

    def sub_block(d, s):
        rs = slice(s * sub, (s + 1) * sub)
        if d == 0:
            return rs, slice(0, (s + 1) * sub), (s * sub - 1 if s > 0 else None)
        return rs, slice(s * sub, cs), ((s + 1) * sub if s < n_sub - 1 else None)

    def causal_mask(d, s):
        _, cols, _ = sub_block(d, s)
        width = cols.stop - cols.start
        row = lax.broadcasted_iota(jnp.int32, (sub, width), 0) + s * sub
        col = lax.broadcasted_iota(jnp.int32, (sub, width), 1) + cols.start
        return row >= col if d == 0 else row <= col

    causal = [[causal_mask(d, s) for s in range(n_sub)] for d in range(2)]

    def body(i, carry):
        for d in range(2):
            c = jnp.where(i < ncc, nxc + i, i - ncc) if d == 0 else nxc + ncc - 1 - i
            r0 = pl.multiple_of(c * cs, cs)
            g = gl_ref[d, pl.ds(r0, cs), :]
            cum = _tri_dot(incl[d], g)
            q = jnp.concatenate([q_ref[t, pl.ds(r0, cs), :] for t in range(2)], axis=1)
            k = jnp.concatenate([k_ref[t, pl.ds(r0, cs), :] for t in range(2)], axis=1)
            v = jnp.concatenate([v_ref[t, pl.ds(r0, cs), :] for t in range(4)], axis=1)
            clast = cum[cs - 1:cs, :] if d == 0 else cum[0:1, :]
            st = st_ref[d]
            o_inter = _dot(q * (jnp.exp(cum) * scale), st, NT_DIMS)
            parts = []
            for s in range(n_sub):
                rs, cols, ref_row = sub_block(d, s)
                ref = cum[ref_row:ref_row + 1, :] if ref_row is not None else 0.0
                qs = q[rs] * (jnp.exp(cum[rs] - ref) * scale)
                ks = k[cols] * jnp.exp(ref - cum[cols])
                a = jnp.where(causal[d][s], _dot(qs, ks, NT_DIMS), 0.0)
                parts.append(_dot(a, v[cols]))
            o = o_inter + jnp.concatenate(parts, axis=0)
            st_ref[d] = st * jnp.exp(clast) + _dot(v, k * jnp.exp(clast - cum), TN_DIMS)

            @pl.when(c < nxc)
            def _():
                acc_ref[pl.ds(r0, cs), :] += o
        return carry

    n_steps = rows // cs
    lax.fori_loop(0, n_steps, body, 0, unroll=2 if n_steps % 2 == 0 else 1)
    o = acc_ref[...]
    y = o * lax.rsqrt(jnp.mean(o * o, axis=-1, keepdims=True) + NORM_EPS) * nw_ref[...]
    for t in range(GLA_DV // LANES):
        r = r_ref[t, 0:n_x, :].astype(F32)
        o_ref[t] = (y[:, t * LANES:(t + 1) * LANES] * _silu(r)).astype(o_ref.dtype)


def _gla_mixer(p, small, gate_w2, gate_b, norm_w, nb, n_x, n_c):
    rows = n_x + n_c
    w2p = jnp.zeros((2, LANES, GLA_HEADS * GLA_DK), F32)
    for d in range(2):
        w2p = w2p.at[d, d * GLA_RANK:(d + 1) * GLA_RANK].set(gate_w2[d].astype(F32))
    qt, vt = GLA_DK // LANES, GLA_DV // LANES
    nq = GLA_HEADS
    return pl.pallas_call(
        functools.partial(_gla_kernel, n_x=n_x, n_c=n_c),
        grid=(nb, GLA_HEADS),
        in_specs=[pl.BlockSpec((qt, rows, LANES), lambda b, h: (h, b, 0), pipeline_mode=pl.Buffered(1)),
                  pl.BlockSpec((qt, rows, LANES), lambda b, h: (nq + h, b, 0), pipeline_mode=pl.Buffered(1)),
                  pl.BlockSpec((vt, rows, LANES), lambda b, h: (nq + h, b, 0), pipeline_mode=pl.Buffered(1)),
                  pl.BlockSpec((vt, rows, LANES), lambda b, h: (2 * nq + h, b, 0), pipeline_mode=pl.Buffered(1)),
                  pl.BlockSpec((rows, LANES), lambda b, h: (b, 0)),
                  pl.BlockSpec((2, LANES, GLA_DK), lambda b, h: (0, 0, h)),
                  pl.BlockSpec((2, 1, GLA_DK), lambda b, h: (0, 0, h)),
                  pl.BlockSpec((1, GLA_DV), lambda b, h: (0, 0))],
        out_specs=pl.BlockSpec((vt, n_x, LANES), lambda b, h: (h, b, 0)),
        out_shape=jax.ShapeDtypeStruct((GLA_HEADS * vt, nb * n_x, LANES), BF16),
        scratch_shapes=[pltpu.VMEM((2, rows, GLA_DK), F32),
                        pltpu.VMEM((n_x, GLA_DV), F32),
                        pltpu.VMEM((2, GLA_DV, GLA_DK), F32)],
        compiler_params=_cparams("arbitrary", "arbitrary"),
        name="gla",
    )(p, p, p, p, small, w2p, gate_b.astype(F32).reshape(2, 1, GLA_HEADS * GLA_DK),
      norm_w.astype(F32).reshape(1, GLA_DV))


def _router_kernel(h_ref, w_ref, e_ref, g_ref):
    logits = _dot(h_ref[...], w_ref[...])
    lane = lax.broadcasted_iota(jnp.int32, logits.shape, 1).astype(F32)
    logits = jnp.where(lane < N_EXPERTS, logits, NEG_BIG)
    m1 = jnp.max(logits, axis=-1, keepdims=True)
    i1 = jnp.min(jnp.where(logits == m1, lane, float(LANES)), axis=-1, keepdims=True)
    rest = jnp.where(lane == i1, NEG_BIG, logits)
    m2 = jnp.max(rest, axis=-1, keepdims=True)
    i2 = jnp.min(jnp.where(rest == m2, lane, float(LANES)), axis=-1, keepdims=True)
    e2 = jnp.exp(m2 - m1)
    den = 1.0 + e2
    e_ref[...] = jnp.where(lane == 0.0, i1, jnp.where(lane == 1.0, i2, 0.0)).astype(jnp.int32)
    g_ref[...] = jnp.where(lane == 0.0, 1.0 / den, jnp.where(lane == 1.0, e2 / den, 0.0))


def _router(h, w_router):
    m = h.shape[0]
    tm = 512
    wp = jnp.pad(w_router.astype(F32), ((0, 0), (0, LANES - N_EXPERTS)))
    return pl.pallas_call(
        _router_kernel,
        grid=(m // tm,),
        in_specs=[pl.BlockSpec((tm, D_MODEL), lambda i: (i, 0)),
                  pl.BlockSpec((D_MODEL, LANES), lambda i: (0, 0))],
        out_specs=[pl.BlockSpec((tm, LANES), lambda i: (i, 0)),
                   pl.BlockSpec((tm, LANES), lambda i: (i, 0))],
        out_shape=[jax.ShapeDtypeStruct((m, LANES), jnp.int32),
                   jax.ShapeDtypeStruct((m, LANES), F32)],
        compiler_params=_cparams("arbitrary"),
        name="router",
    )(h, wp)


def _row_copy(src_ref, src_row, dst_ref, dst_row, sem):
    return pltpu.make_async_copy(src_ref.at[pl.ds(src_row, 1)], dst_ref.at[pl.ds(dst_row, 1)], sem)


def _moe_kernel(be_ref, nused_ref, tok_ref, dst_ref, h_ref, w1_ref, w3_ref, w2_ref, y_ref,
                a32_ref, a16_ref, acc_ref, sem):
    i = pl.program_id(0)
    j = pl.program_id(1)
    nj = pl.num_programs(1)
    base = i * MOE_TM

    @pl.when(i < nused_ref[0])
    def _():
        @pl.when(j == 0)
        def _():
            def start(r, c):
                _row_copy(h_ref, tok_ref[base + r], a32_ref, r, sem).start()
                return c

            def wait(r, c):
                _row_copy(h_ref, tok_ref[base + r], a32_ref, r, sem).wait()
                return c

            lax.fori_loop(0, MOE_TM, start, 0, unroll=8)
            lax.fori_loop(0, MOE_TM, wait, 0, unroll=8)
            a16_ref[...] = a32_ref[...].astype(BF16)

        a = a16_ref[...]
        h1 = jnp.dot(a, w1_ref[0].astype(BF16), preferred_element_type=F32)
        h3 = jnp.dot(a, w3_ref[0].astype(BF16), preferred_element_type=F32)
        hh = (_silu(h1) * h3).astype(BF16)
        @pl.when(j == 0)
        def _():
            acc_ref[...] = jnp.zeros_like(acc_ref)

        _accumulate(acc_ref, hh, w2_ref[0].astype(BF16))

        @pl.when(j == nj - 1)
        def _():
            def start(r, c):
                dst = dst_ref[base + r]

                @pl.when(dst >= 0)
                def _():
                    _row_copy(acc_ref, r, y_ref, dst, sem).start()
                return c

            def wait(r, c):
                dst = dst_ref[base + r]

                @pl.when(dst >= 0)
                def _():
                    _row_copy(acc_ref, r, y_ref, dst, sem).wait()
                return c

            lax.fori_loop(0, MOE_TM, start, 0, unroll=8)
            lax.fori_loop(0, MOE_TM, wait, 0, unroll=8)


def _moe(h, experts, w1, w3, w2):
    n_tok = h.shape[0]
    n_asg = n_tok * TOP_K
    n_blocks = -(-n_asg // MOE_TM) + N_EXPERTS
    n_rows = n_blocks * MOE_TM
    th = 256
    nj = EXPERT_HIDDEN // th
    flat_e = experts.reshape(n_asg)
    onehot = (flat_e[:, None] == jnp.arange(N_EXPERTS, dtype=jnp.int32)[None, :]).astype(jnp.int32)
    csum = jnp.cumsum(onehot, axis=0)
    rank = jnp.sum(csum * onehot, axis=1) - 1
    counts = csum[-1]
    padded = (counts + MOE_TM - 1) // MOE_TM * MOE_TM
    padded_end = jnp.cumsum(padded)
    padded_start = padded_end - padded
    dest = padded_start[flat_e] + rank
    asg = jnp.arange(n_asg, dtype=jnp.int32)
    row_tok = jnp.zeros((n_rows,), jnp.int32).at[dest].set(asg // TOP_K)
    row_dst = jnp.full((n_rows,), -1, jnp.int32).at[dest].set(asg)
    blk_start = jnp.arange(n_blocks, dtype=jnp.int32) * MOE_TM
    n_used = (padded_end[-1] // MOE_TM).astype(jnp.int32)
    blk_e = jnp.sum((padded_end[None, :] <= blk_start[:, None]).astype(jnp.int32), axis=1)
    last_e = jnp.sum((padded_end <= (n_used - 1) * MOE_TM).astype(jnp.int32))
    blk_e = jnp.where(blk_start < padded_end[-1], blk_e, last_e).astype(jnp.int32)

    def w_col(i, j, be, nu, *_):
        return jnp.where(i < nu[0], j, nj - 1)

    grid_spec = pltpu.PrefetchScalarGridSpec(
        num_scalar_prefetch=4,
        grid=(n_blocks, nj),
        in_specs=[pl.BlockSpec(memory_space=pl.ANY),
                  pl.BlockSpec((1, D_MODEL, th), lambda i, j, be, *s: (be[i], 0, w_col(i, j, be, *s))),
                  pl.BlockSpec((1, D_MODEL, th), lambda i, j, be, *s: (be[i], 0, w_col(i, j, be, *s))),
                  pl.BlockSpec((1, th, D_MODEL), lambda i, j, be, *s: (be[i], w_col(i, j, be, *s), 0))],
        out_specs=pl.BlockSpec(memory_space=pl.ANY),
        scratch_shapes=[pltpu.VMEM((MOE_TM, D_MODEL), F32),
                        pltpu.VMEM((MOE_TM, D_MODEL), BF16),
                        pltpu.VMEM((MOE_TM, D_MODEL), F32),
                        pltpu.SemaphoreType.DMA(())],
    )
    return pl.pallas_call(
        _moe_kernel,
        grid_spec=grid_spec,
        out_shape=jax.ShapeDtypeStruct((n_asg, D_MODEL), F32),
        compiler_params=_cparams("arbitrary", "arbitrary"),
        name="moe",
    )(blk_e, n_used.reshape(1), row_tok, row_dst, h, w1, w3, w2)


def kernel(x, c, ctx, c_ctx, ada_w, ada_b, ln1_g, ln1_b, ln2_g, ln2_b, e_w_in, s5_lam_re, s5_lam_im, s5_log_step, s5_b_re, s5_b_im, s5_c_re, s5_c_im, s5_d, dn_conv, dn_a_log, dn_dt_bias, dn_norm_w, e_w_out, ffn_w1, ffn_w3, ffn_w2, o_w_in, gla_w2, gla_b, gla_norm_w, o_w_out, moe_router, moe_w1, moe_w3, moe_w2):
    nb, n_x, _ = x.shape
    n_c = ctx.shape[1]
    rows = n_x + n_c
    m_all = nb * rows
    assert nb < 8 and n_x % (GRID_W * 8) == 0 and n_c % ROW_BLK == 0 and n_x % ROW_BLK == 0

    c8 = jnp.zeros((8, D_MODEL), F32).at[:nb].set(c).at[nb].set(c_ctx)
    mods = _adaln(c8, ada_w, ada_b).reshape(DEPTH, 8, N_MOD, D_MODEL)
    xc = jnp.concatenate([x, ctx], axis=1)

    h = _modulate(xc, mods, 0, n_x)
    p, small = _mm_in(h.reshape(m_all, D_MODEL), e_w_in[0])
    tables = _s5_tables(s5_lam_re[0].astype(F32), s5_lam_im[0].astype(F32), s5_log_step[0].astype(F32),
                        s5_b_re[0].astype(F32), s5_b_im[0].astype(F32), s5_c_re[0].astype(F32),
                        s5_c_im[0].astype(F32), s5_d[0])
    u_tile0 = (3 * DN_WIDTH + DN_WIDTH) // LANES
    y_s5 = _s5_mixer(p, tables, nb, n_x // S5_L, n_c // S5_L, u_tile0, u_tile0 + S5_TILES)
    y_dn = _deltanet_mixer(p, small, dn_conv[0], dn_a_log[0], dn_dt_bias[0], dn_norm_w[0], nb, n_x, n_c)
    mix = _mm_out([y_s5, y_dn], e_w_out[0]).reshape(nb, rows, D_MODEL)
    xc, h = _resid_ln(xc, mix, mods, 0, 2, 0, 3, ln1_g[0], ln1_b[0], n_x, rows, h_dtype=BF16)
    f = _ffn(h.reshape(m_all, D_MODEL), ffn_w1[0], ffn_w3[0], ffn_w2[0]).reshape(nb, rows, D_MODEL)
    xc, h = _resid_ln(xc, f, mods, 0, 5, 1, 0, ln2_g[0], ln2_b[0], n_x, rows, h_dtype=BF16)

    p, small = _mm_in(h.reshape(m_all, D_MODEL), o_w_in[0])
    y_gla = _gla_mixer(p, small, gla_w2[0], gla_b[0], gla_norm_w[0], nb, n_x, n_c)
    mix = _mm_out([y_gla], o_w_out[0]).reshape(nb, n_x, D_MODEL)
    xl, h = _resid_ln(xc, mix, mods, 1, 2, 1, 3, ln1_g[1], ln1_b[1], n_x, n_x, h_dtype=F32)
    hf = h.reshape(nb * n_x, D_MODEL)
    experts, gates = _router(hf, moe_router[0])
    y = _moe(hf, experts[:, :TOP_K], moe_w1[0], moe_w3[0], moe_w2[0])
    out, _ = _resid_ln(xl, y.reshape(nb, n_x, TOP_K * D_MODEL), mods, 1, 5, 1, 0, ln2_g[1], ln2_b[1], n_x, n_x,
                       moe_gates=gates[:, :TOP_K].reshape(nb, n_x, TOP_K))
    return out
```
